```python
import math
import jax, jax.numpy as jnp
from jax import lax
import numpy as np

D_MODEL = 1024
BATCH = 16
SEQ = 2048
DEPTH = 4

CHUNK = 64
N_MIXERS = 2
N_GDN = (DEPTH + 1) // 2
N_HGRN = DEPTH // 2
EPS = 1e-6

GDN_HEADS = 8
GDN_HEAD_DIM = D_MODEL // GDN_HEADS
GDN_KEY_WIDTH = GDN_HEADS * GDN_HEAD_DIM
GDN_VAL_WIDTH = GDN_HEADS * GDN_HEAD_DIM
GDN_CONV_WIDTH = 2 * GDN_KEY_WIDTH + GDN_VAL_WIDTH
GDN_IN_WIDTH = GDN_CONV_WIDTH + GDN_VAL_WIDTH + 2 * GDN_HEADS
CONV_K = 4

HGRN_FORGET_DIM = 128
HGRN_HEADS = D_MODEL // HGRN_FORGET_DIM
HGRN_VALUE_DIM = D_MODEL // HGRN_HEADS
HGRN_WIDTH = HGRN_HEADS * HGRN_FORGET_DIM
HGRN_IN_WIDTH = 4 * HGRN_WIDTH
SUB = 16
N_SUB = CHUNK // SUB

MLP_HIDDEN = 4 * D_MODEL

kernel_name = "hybrid_gdn_hgrn2_stream_encoder"


def _rmsnorm(x, w):
    xf = x.astype(jnp.float32)
    y = xf * lax.rsqrt(jnp.mean(xf * xf, axis=-1, keepdims=True) + EPS)
    return (y * w.astype(jnp.float32)).astype(x.dtype)


def _l2norm(x):
    return x * lax.rsqrt(jnp.sum(x * x, axis=-1, keepdims=True) + EPS)


def _causal_conv(x, w):
    K = w.shape[0]
    T = x.shape[1]
    xp = jnp.pad(x, ((0, 0), (K - 1, 0), (0, 0)))
    y = xp[:, 0:T] * w[0]
    for kk in range(1, K):
        y = y + xp[:, kk:kk + T] * w[kk]
    return y


def _to_chunks(a, nc):
    B = a.shape[0]
    H = a.shape[2]
    a = a.reshape((B, nc, CHUNK, H) + a.shape[3:])
    return jnp.moveaxis(a, (1, 3), (0, 2))


def _from_chunks(o):
    nc, B, H, C, Dv = o.shape
    return jnp.moveaxis(o, (0, 2), (1, 3)).reshape(B, nc * C, H, Dv)


def _gdn_chunked(q, k, v, beta, g):
    B, T, H, DK = q.shape
    DV = v.shape[-1]
    nc = T // CHUNK
    q = _to_chunks(q * (DK ** -0.5), nc)
    k = _to_chunks(k, nc)
    v = _to_chunks(v, nc)
    beta = _to_chunks(beta, nc)
    gc = jnp.cumsum(_to_chunks(g, nc), axis=-1)
    causal = jnp.tril(jnp.ones((CHUNK, CHUNK), dtype=bool))
    strict = jnp.tril(jnp.ones((CHUNK, CHUNK), dtype=bool), k=-1)
    decay = jnp.exp(jnp.where(causal, gc[..., :, None] - gc[..., None, :], -jnp.inf))
    kb = k * beta[..., None]
    L = jnp.where(strict, jnp.einsum('nbhik,nbhjk->nbhij', kb, k) * decay, 0.0)
    rhs = jnp.concatenate([v * beta[..., None], kb * jnp.exp(gc)[..., None]], axis=-1)
    sol = lax.linalg.triangular_solve(L, rhs, left_side=True, lower=True, unit_diagonal=True)
    u = sol[..., :DV]
    w = sol[..., DV:]
    a_qk = jnp.where(causal, jnp.einsum('nbhik,nbhjk->nbhij', q, k) * decay, 0.0)
    q_dec = q * jnp.exp(gc)[..., None]
    k_dec = k * jnp.exp(gc[..., -1:] - gc)[..., None]
    chunk_decay = jnp.exp(gc[..., -1])

    def step(S, xs):
        qd, a_c, u_c, w_c, kd, dl = xs
        v_new = u_c - jnp.einsum('bhck,bhkv->bhcv', w_c, S)
        o = jnp.einsum('bhck,bhkv->bhcv', qd, S) + jnp.einsum('bhij,bhjv->bhiv', a_c, v_new)
        S = S * dl[..., None, None] + jnp.einsum('bhck,bhcv->bhkv', kd, v_new)
        return S, o

    S0 = jnp.zeros((B, H, DK, DV), jnp.float32)
    _, o = lax.scan(step, S0, (q_dec, a_qk, u, w, k_dec, chunk_decay))
    return _from_chunks(o)


def _hgrn2_chunked(q, k, v, g):
    B, T, H, DK = q.shape
    DV = v.shape[-1]
    nc = T // CHUNK
    q = _to_chunks(q * (DK ** -0.5), nc)
    k = _to_chunks(k, nc)
    v = _to_chunks(v, nc)
    g = _to_chunks(g, nc)
    gc = jnp.cumsum(g, axis=-2)
    gb = (gc - g)[:, :, :, ::SUB]
    pos = jnp.arange(CHUNK)
    off_mask = (pos[None, :] // SUB) < jnp.arange(N_SUB)[:, None]
    diag_mask = jnp.tril(jnp.ones((SUB, SUB), dtype=bool))

    def step(S, xs):
        q_c, k_c, v_c, gc_c, gb_c = xs
        qs = q_c.reshape(B, H, N_SUB, SUB, DK)
        ks = k_c.reshape(B, H, N_SUB, SUB, DK)
        vs = v_c.reshape(B, H, N_SUB, SUB, DV)
        gs = gc_c.reshape(B, H, N_SUB, SUB, DK)
        q_off = qs * jnp.exp(gs - gb_c[:, :, :, None])
        k_off = k_c[:, :, None] * jnp.exp(
            jnp.where(off_mask[:, :, None], gb_c[:, :, :, None] - gc_c[:, :, None], -jnp.inf))
        a_off = jnp.einsum('bhsik,bhsjk->bhsij', q_off, k_off)
        dec = jnp.exp(jnp.where(diag_mask[:, :, None],
                                gs[..., :, None, :] - gs[..., None, :, :], -jnp.inf))
        a_diag = jnp.einsum('bhsik,bhsjk,bhsijk->bhsij', qs, ks, dec)
        o = (jnp.einsum('bhsij,bhjv->bhsiv', a_off, v_c)
             + jnp.einsum('bhsij,bhsjv->bhsiv', a_diag, vs)).reshape(B, H, CHUNK, DV)
        o = o + jnp.einsum('bhck,bhkv->bhcv', q_c * jnp.exp(gc_c), S)
        g_last = gc_c[:, :, -1]
        S = S * jnp.exp(g_last)[..., None] + jnp.einsum(
            'bhck,bhcv->bhkv', k_c * jnp.exp(g_last[:, :, None] - gc_c), v_c)
        return S, o

    S0 = jnp.zeros((B, H, DK, DV), jnp.float32)
    _, o = lax.scan(step, S0, (q, k, v, gc, gb))
    return _from_chunks(o)


def _gated_deltanet(y, w_in, conv_w, a_log, dt_bias, onorm_w, w_out):
    B, T, _ = y.shape
    f32 = jnp.float32
    proj = y @ w_in
    qkv = proj[..., :GDN_CONV_WIDTH]
    gate = proj[..., GDN_CONV_WIDTH:GDN_CONV_WIDTH + GDN_VAL_WIDTH]
    a = proj[..., GDN_CONV_WIDTH + GDN_VAL_WIDTH:GDN_CONV_WIDTH + GDN_VAL_WIDTH + GDN_HEADS]
    b = proj[..., GDN_CONV_WIDTH + GDN_VAL_WIDTH + GDN_HEADS:]
    qkv = jax.nn.silu(_causal_conv(qkv, conv_w)).astype(f32)
    q = _l2norm(qkv[..., :GDN_KEY_WIDTH].reshape(B, T, GDN_HEADS, GDN_HEAD_DIM))
    k = _l2norm(qkv[..., GDN_KEY_WIDTH:2 * GDN_KEY_WIDTH].reshape(B, T, GDN_HEADS, GDN_HEAD_DIM))
    v = qkv[..., 2 * GDN_KEY_WIDTH:].reshape(B, T, GDN_HEADS, GDN_HEAD_DIM)
    beta = jax.nn.sigmoid(b.astype(f32))
    g = -jnp.exp(a_log.astype(f32)) * jax.nn.softplus(a.astype(f32) + dt_bias.astype(f32))
    o = _gdn_chunked(q, k, v, beta, g)
    o = _rmsnorm(o, onorm_w) * jax.nn.silu(gate.astype(f32)).reshape(B, T, GDN_HEADS, GDN_HEAD_DIM)
    return o.reshape(B, T, GDN_VAL_WIDTH).astype(y.dtype) @ w_out


def _hgrn2(y, w_in, lb, gnorm_w, w_out):
    B, T, _ = y.shape
    f32 = jnp.float32
    proj = y @ w_in
    q, f, i, gate = jnp.split(proj, 4, axis=-1)
    f = f.astype(f32)
    log_forget = jnp.logaddexp(jnp.log(lb), jnp.log1p(-lb) + jax.nn.log_sigmoid(f))
    k = (1.0 - lb) * jax.nn.sigmoid(-f)
    q = jax.nn.silu(q.astype(f32))
    heads = lambda t: t.reshape(B, T, HGRN_HEADS, -1)
    o = _hgrn2_chunked(heads(q), heads(k), heads(i.astype(f32)), heads(log_forget))
    o = _rmsnorm(o.reshape(B, T, HGRN_WIDTH), gnorm_w) * jax.nn.silu(gate.astype(f32))
    return o.astype(y.dtype) @ w_out


def _sq_relu_mlp(h, w_up, w_down):
    return jnp.square(jax.nn.relu(h @ w_up)) @ w_down


def setup_inputs(seed: int = 0) -> dict:
    key = jax.random.key(seed)
    ks = jax.random.split(key, 17)
    f32 = jnp.float32
    nrm = lambda kk, shape: jax.random.normal(kk, shape, f32)
    x = nrm(ks[0], (BATCH, SEQ, D_MODEL))
    gdn_w_in = nrm(ks[1], (N_GDN, D_MODEL, GDN_IN_WIDTH)) * D_MODEL ** -0.5
    gdn_conv = nrm(ks[2], (N_GDN, CONV_K, GDN_CONV_WIDTH)) * CONV_K ** -0.5
    gdn_a_log = jnp.log(jax.random.uniform(ks[3], (N_GDN, GDN_HEADS), f32, 1.0, 16.0))
    dt = jnp.exp(jax.random.uniform(ks[4], (N_GDN, GDN_HEADS), f32, math.log(1e-3), math.log(1e-1)))
    gdn_dt_bias = dt + jnp.log(-jnp.expm1(-dt))
    gdn_onorm = 1.0 + 0.02 * nrm(ks[5], (N_GDN, GDN_HEAD_DIM))
    gdn_w_out = nrm(ks[6], (N_GDN, GDN_VAL_WIDTH, D_MODEL)) * GDN_VAL_WIDTH ** -0.5
    hgrn_w_in = nrm(ks[7], (N_HGRN, D_MODEL, HGRN_IN_WIDTH)) * D_MODEL ** -0.5
    hgrn_lb_logits = 0.1 * nrm(ks[8], (DEPTH, HGRN_WIDTH))
    hgrn_gnorm = 1.0 + 0.02 * nrm(ks[9], (N_HGRN, HGRN_WIDTH))
    hgrn_w_out = nrm(ks[10], (N_HGRN, HGRN_WIDTH, D_MODEL)) * HGRN_WIDTH ** -0.5
    norm_mix = 1.0 + 0.02 * nrm(ks[11], (DEPTH, D_MODEL))
    norm_mlp = 1.0 + 0.02 * nrm(ks[12], (DEPTH, D_MODEL))
    mlp_w_up = nrm(ks[13], (DEPTH, D_MODEL, MLP_HIDDEN)) * D_MODEL ** -0.5
    mlp_w_down = nrm(ks[14], (DEPTH, MLP_HIDDEN, D_MODEL)) * MLP_HIDDEN ** -0.5
    norm_final = 1.0 + 0.02 * nrm(ks[15], (D_MODEL,))
    return {"x": x, "gdn_w_in": gdn_w_in, "gdn_conv": gdn_conv, "gdn_a_log": gdn_a_log,
            "gdn_dt_bias": gdn_dt_bias, "gdn_onorm": gdn_onorm, "gdn_w_out": gdn_w_out,
            "hgrn_w_in": hgrn_w_in, "hgrn_lb_logits": hgrn_lb_logits, "hgrn_gnorm": hgrn_gnorm,
            "hgrn_w_out": hgrn_w_out, "norm_mix": norm_mix, "norm_mlp": norm_mlp,
            "mlp_w_up": mlp_w_up, "mlp_w_down": mlp_w_down, "norm_final": norm_final}


def reference(x, gdn_w_in, gdn_conv, gdn_a_log, gdn_dt_bias, gdn_onorm, gdn_w_out,
              hgrn_w_in, hgrn_lb_logits, hgrn_gnorm, hgrn_w_out, norm_mix, norm_mlp,
              mlp_w_up, mlp_w_down, norm_final):
    sm = jax.nn.softmax(hgrn_lb_logits.astype(jnp.float32), axis=0)
    lower_bounds = jnp.cumsum(sm, axis=0) - sm[0]
    h = x
    for i in range(DEPTH):
        j = i // N_MIXERS
        y = _rmsnorm(h, norm_mix[i])
        if i % N_MIXERS == 0:
            y = _gated_deltanet(y, gdn_w_in[j], gdn_conv[j], gdn_a_log[j], gdn_dt_bias[j],
                                gdn_onorm[j], gdn_w_out[j])
        else:
            y = _hgrn2(y, hgrn_w_in[j], lower_bounds[i], hgrn_gnorm[j], hgrn_w_out[j])
        h = h + y.astype(h.dtype)
        h = h + _sq_relu_mlp(_rmsnorm(h, norm_mlp[i]), mlp_w_up[i], mlp_w_down[i]).astype(h.dtype)
    return _rmsnorm(h, norm_final)
```

```python
import functools

import numpy as np
import jax
import jax.numpy as jnp
from jax import lax
from jax.experimental import pallas as pl
from jax.experimental.pallas import tpu as pltpu

F32 = jnp.float32
BF16 = jnp.bfloat16

D_MODEL = 1024
HEADS = 8
HEAD_DIM = 128
CHUNK = 64
SUB = 16
N_SUB = CHUNK // SUB
CONV_K = 4
MLP_HIDDEN = 4 * D_MODEL
EPS = 1e-6
LANES = 128
HALO_ROWS = 8
GDN_PROJ_WIDTH = 4 * D_MODEL + LANES
V7X_VMEM_LIMIT_BYTES = 56 * 1024 * 1024

ROWS_IN = 256
ROWS_REC = 256
ROWS_OUT = 256


def _resident(shape):
    nd = len(shape)
    return pl.BlockSpec(shape, lambda *_: (0,) * nd, pipeline_mode=pl.Buffered(1))


def _params(n_axes):
    return pltpu.CompilerParams(
        dimension_semantics=("arbitrary",) * n_axes,
        vmem_limit_bytes=V7X_VMEM_LIMIT_BYTES)


def _dot(a, b):
    return jnp.dot(a.astype(BF16), b.astype(BF16), preferred_element_type=F32)


def _dot_nt(a, b):
    return lax.dot_general(a.astype(BF16), b.astype(BF16),
                           (((1,), (1,)), ((), ())), preferred_element_type=F32)


def _dot_tn(a, b):
    return lax.dot_general(a.astype(BF16), b.astype(BF16),
                           (((0,), (0,)), ((), ())), preferred_element_type=F32)


def _split2(x):
    hi = x.astype(BF16)
    lo = (x - hi.astype(F32)).astype(BF16)
    return hi, lo


def _dot_split(a, b):
    a_hi, a_lo = _split2(a)
    b_hi, b_lo = _split2(b)
    d = functools.partial(jnp.dot, preferred_element_type=F32)
    return d(a_hi, b_hi) + (d(a_hi, b_lo) + d(a_lo, b_hi))


def _cumsum_rows(tril_bf16, x):
    x1 = x.astype(BF16)
    r1 = x - x1.astype(F32)
    x2 = r1.astype(BF16)
    x3 = (r1 - x2.astype(F32)).astype(BF16)
    d = functools.partial(jnp.dot, preferred_element_type=F32)
    return d(tril_bf16, x1) + (d(tril_bf16, x2) + d(tril_bf16, x3))


def _rms(x, w):
    ms = jnp.mean(x * x, axis=-1, keepdims=True)
    return x * lax.rsqrt(ms + EPS) * w


def _sigmoid(x):
    return 1.0 / (1.0 + jnp.exp(-x))


def _silu(x):
    return x * _sigmoid(x)


def _softplus(x):
    return jnp.maximum(x, 0.0) + jnp.log1p(jnp.exp(-jnp.abs(x)))


def _gdn_in_kernel(h_ref, nw_ref, w_ref, cw_ref, alog_ref, dtb_ref,
                   q_ref, k_ref, v_ref, sg_ref, g_ref, b_ref,
                   pbuf, halo, *, rows, tiles_per_seq):
    @pl.when(pl.program_id(0) % tiles_per_seq == 0)
    def _():
        halo[...] = jnp.zeros_like(halo)

    xn = _rms(h_ref[...], nw_ref[...]).astype(BF16)

    for ci, out_ref in enumerate((q_ref, k_ref, v_ref)):
        c0 = ci * D_MODEL
        p = jnp.dot(xn, w_ref[:, c0:c0 + D_MODEL], preferred_element_type=F32)
        pbuf[0:HALO_ROWS, :] = halo[:, c0:c0 + D_MODEL]
        pbuf[HALO_ROWS:HALO_ROWS + rows, :] = p
        halo[:, c0:c0 + D_MODEL] = p[rows - HALO_ROWS:rows, :]
        acc = cw_ref[0:1, c0:c0 + D_MODEL] * pbuf[HALO_ROWS - 3:HALO_ROWS - 3 + rows, :]
        for tap in range(1, CONV_K - 1):
            start = HALO_ROWS - (CONV_K - 1) + tap
            acc = acc + cw_ref[tap:tap + 1, c0:c0 + D_MODEL] * pbuf[start:start + rows, :]
        acc = acc + cw_ref[CONV_K - 1:CONV_K, c0:c0 + D_MODEL] * p
        s = _silu(acc)
        for h in range(HEADS):
            sl = s[:, h * HEAD_DIM:(h + 1) * HEAD_DIM]
            if ci < 2:
                ss = jnp.sum(sl * sl, axis=-1, keepdims=True)
                sl = sl * lax.rsqrt(ss + EPS)
                if ci == 0:
                    sl = sl * (HEAD_DIM ** -0.5)
            out_ref[h] = sl

    pg = jnp.dot(xn, w_ref[:, 3 * D_MODEL:4 * D_MODEL], preferred_element_type=F32)
    sg_ref[...] = _silu(pg)

    pab = jnp.dot(xn, w_ref[:, 4 * D_MODEL:GDN_PROJ_WIDTH], preferred_element_type=F32)
    gv = -jnp.exp(alog_ref[...]) * _softplus(pab + dtb_ref[...])
    bv = _sigmoid(pab)
    for h in range(HEADS):
        g_ref[h] = jnp.broadcast_to(gv[:, h:h + 1], (rows, LANES))
        b_ref[h] = jnp.broadcast_to(bv[:, HEADS + h:HEADS + h + 1], (rows, LANES))


def _gdn_in(h, norm_w, w_in, conv_w, a_log, dt_bias, seq_len):
    n = h.shape[0]
    rows = ROWS_IN
    w = jnp.pad(w_in, ((0, 0), (0, GDN_PROJ_WIDTH - w_in.shape[1]))).astype(BF16)
    alog = jnp.pad(a_log.astype(F32), (0, LANES - HEADS)).reshape(1, LANES)
    dtb = jnp.pad(dt_bias.astype(F32), (0, LANES - HEADS)).reshape(1, LANES)
    head_major = jax.ShapeDtypeStruct((HEADS, n, HEAD_DIM), F32)
    hm_spec = pl.BlockSpec((HEADS, rows, HEAD_DIM), lambda i: (0, i, 0))
    row_spec = pl.BlockSpec((rows, D_MODEL), lambda i: (i, 0))
    return pl.pallas_call(
        functools.partial(_gdn_in_kernel, rows=rows, tiles_per_seq=seq_len // rows),
        grid=(n // rows,),
        in_specs=[row_spec, _resident((1, D_MODEL)), _resident((D_MODEL, GDN_PROJ_WIDTH)),
                  _resident((CONV_K, 3 * D_MODEL)), _resident((1, LANES)), _resident((1, LANES))],
        out_specs=[hm_spec, hm_spec, hm_spec, row_spec, hm_spec, hm_spec],
        out_shape=[head_major, head_major, head_major,
                   jax.ShapeDtypeStruct((n, D_MODEL), F32), head_major, head_major],
        scratch_shapes=[pltpu.VMEM((HALO_ROWS + rows, D_MODEL), F32),
                        pltpu.VMEM((HALO_ROWS, 3 * D_MODEL), F32)],
        compiler_params=_params(1),
        name="gdn_in",
    )(h, norm_w.reshape(1, D_MODEL), w, conv_w, alog, dtb)


def _gdn_rec_kernel(q_ref, k_ref, v_ref, g_ref, b_ref, o_ref, s_ref, *, n_chunks):
    @pl.when(pl.program_id(1) == 0)
    def _():
        s_ref[...] = jnp.zeros_like(s_ref)

    row = lax.broadcasted_iota(jnp.int32, (CHUNK, CHUNK), 0)
    col = lax.broadcasted_iota(jnp.int32, (CHUNK, CHUNK), 1)
    causal = row >= col
    strict = row > col
    tril = jnp.where(causal, 1.0, 0.0).astype(BF16)
    eye = jnp.where(row == col, 1.0, 0.0).astype(F32)
    zpad = jnp.zeros((LANES - CHUNK, LANES), F32)

    def body(c, carry):
        r0 = pl.multiple_of(c * CHUNK, CHUNK)
        rs = pl.ds(r0, CHUNK)
        for h in range(HEADS):
            q = q_ref[h, rs, :]
            k = k_ref[h, rs, :]
            v = v_ref[h, rs, :]
            beta = b_ref[h, rs, :]
            gc = _cumsum_rows(tril, g_ref[h, rs, :])
            gc_row = jnp.concatenate([gc, zpad], axis=0).T[:CHUNK, :CHUNK]
            decay = jnp.exp(jnp.where(causal, gc[:, :CHUNK] - gc_row, -jnp.inf))
            kb = k * beta
            lmat = jnp.where(strict, _dot_nt(kb, k) * decay, 0.0)
            a_qk = jnp.where(causal, _dot_nt(q, k) * decay, 0.0)
            egc = jnp.exp(gc)
            rhs = jnp.concatenate([v * beta, kb * egc], axis=1)
            pw = -lmat
            inv = eye + pw
            for _ in range(5):
                pw = _dot_split(pw, pw)
                inv = inv + _dot_split(inv, pw)
            sol = _dot_split(inv, rhs)
            u = sol[:, :HEAD_DIM]
            w = sol[:, HEAD_DIM:]
            state = s_ref[h]
            v_new = u - _dot(w, state)
            o_ref[h, rs, :] = _dot(q * egc, state) + _dot(a_qk, v_new)
            g_last = gc[CHUNK - 1:CHUNK, :]
            k_dec = k * jnp.exp(g_last - gc)
            s_ref[h] = state * jnp.exp(g_last) + _dot_tn(k_dec, v_new)
        return carry

    lax.fori_loop(0, n_chunks, body, 0)


def _gdn_rec(q, k, v, g, beta, batch, seq_len):
    rows = ROWS_REC
    spec = pl.BlockSpec((HEADS, rows, HEAD_DIM),
                        lambda b, t: (0, b * (seq_len // rows) + t, 0))
    return pl.pallas_call(
        functools.partial(_gdn_rec_kernel, n_chunks=rows // CHUNK),
        grid=(batch, seq_len // rows),
        in_specs=[spec] * 5,
        out_specs=spec,
        out_shape=jax.ShapeDtypeStruct(q.shape, F32),
        scratch_shapes=[pltpu.VMEM((HEADS, HEAD_DIM, HEAD_DIM), F32)],
        compiler_params=_params(2),
        name="gdn_rec",
    )(q, k, v, g, beta)


def _hgrn_in_kernel(h_ref, nw_ref, w_ref, lg_ref,
                    q_ref, k_ref, v_ref, g_ref, sg_ref, *, layer):
    lg = lg_ref[...]
    m = jnp.max(lg, axis=0, keepdims=True)
    e = jnp.exp(lg - m)
    sm = e / jnp.sum(e, axis=0, keepdims=True)
    cs = sm[0:1, :]
    for r in range(1, layer + 1):
        cs = cs + sm[r:r + 1, :]
    lb = cs - sm[0:1, :]
    log_lb = jnp.log(lb)
    log_1m_lb = jnp.log1p(-lb)

    xn = _rms(h_ref[...], nw_ref[...]).astype(BF16)

    def proj(i):
        return jnp.dot(xn, w_ref[:, i * D_MODEL:(i + 1) * D_MODEL], preferred_element_type=F32)

    def heads_out(ref, val):
        for h in range(HEADS):
            ref[h] = val[:, h * HEAD_DIM:(h + 1) * HEAD_DIM]

    heads_out(q_ref, _silu(proj(0)) * (HEAD_DIM ** -0.5))

    f = proj(1)
    t = jnp.exp(-jnp.abs(f))
    log_sig = jnp.minimum(f, 0.0) - jnp.log1p(t)
    b = log_1m_lb + log_sig
    hi = jnp.maximum(log_lb, b)
    heads_out(g_ref, hi + jnp.log1p(jnp.exp(-jnp.abs(log_lb - b))))
    inv = 1.0 / (1.0 + t)
    heads_out(k_ref, (1.0 - lb) * jnp.where(f >= 0.0, t * inv, inv))

    heads_out(v_ref, proj(2))
    sg_ref[...] = _silu(proj(3))


def _hgrn_in(h, norm_w, w_in, lb_logits, layer):
    n = h.shape[0]
    rows = ROWS_IN
    depth = lb_logits.shape[0]
    head_major = jax.ShapeDtypeStruct((HEADS, n, HEAD_DIM), F32)
    hm_spec = pl.BlockSpec((HEADS, rows, HEAD_DIM), lambda i: (0, i, 0))
    row_spec = pl.BlockSpec((rows, D_MODEL), lambda i: (i, 0))
    return pl.pallas_call(
        functools.partial(_hgrn_in_kernel, layer=layer),
        grid=(n // rows,),
        in_specs=[row_spec, _resident((1, D_MODEL)), _resident((D_MODEL, 4 * D_MODEL)),
                  _resident((depth, D_MODEL))],
        out_specs=[hm_spec, hm_spec, hm_spec, hm_spec, row_spec],
        out_shape=[head_major] * 4 + [jax.ShapeDtypeStruct((n, D_MODEL), F32)],
        compiler_params=_params(1),
        name="hgrn_in",
    )(h, norm_w.reshape(1, D_MODEL), w_in.astype(BF16), lb_logits.astype(F32))


def _hgrn_rec_kernel(q_ref, k_ref, v_ref, g_ref, ecol_ref, o_ref, s_ref, *, n_chunks):
    @pl.when(pl.program_id(1) == 0)
    def _():
        s_ref[...] = jnp.zeros_like(s_ref)

    row = lax.broadcasted_iota(jnp.int32, (CHUNK, CHUNK), 0)
    col = lax.broadcasted_iota(jnp.int32, (CHUNK, CHUNK), 1)
    tril = jnp.where(row >= col, 1.0, 0.0).astype(BF16)
    same_sub = (row // SUB) == (col // SUB)
    sub_i = lax.broadcasted_iota(jnp.int32, (N_SUB, SUB, HEAD_DIM), 1)

    def head_chunk(h, c):
        rs = pl.ds(pl.multiple_of(c * CHUNK, CHUNK), CHUNK)
        q = q_ref[h, rs, :]
        k = k_ref[h, rs, :]
        v = v_ref[h, rs, :]
        g = g_ref[h, rs, :]
        gc = _cumsum_rows(tril, g)

        g3 = gc.reshape(N_SUB, SUB, HEAD_DIM)
        q3 = q.reshape(N_SUB, SUB, HEAD_DIM)
        k3 = k.reshape(N_SUB, SUB, HEAD_DIM)
        pieces = []
        for j in range(SUB):
            dec = jnp.exp(jnp.where(sub_i >= j, g3 - g3[:, j:j + 1, :], -jnp.inf))
            pieces.append((q3 * dec * k3[:, j:j + 1, :]).reshape(CHUNK, HEAD_DIM).astype(BF16))
        a_mat = jnp.where(
            same_sub,
            jnp.dot(jnp.concatenate(pieces, axis=1), ecol_ref[...], preferred_element_type=F32),
            0.0)

        gb = gc - g
        a_rows = [jnp.zeros((SUB, CHUNK), F32)]
        for s in range(1, N_SUB):
            lo, hi = s * SUB, (s + 1) * SUB
            gb_s = gb[lo:lo + 1, :]
            q_off = q[lo:hi, :] * jnp.exp(gc[lo:hi, :] - gb_s)
            k_off = jnp.concatenate(
                [k[:lo, :] * jnp.exp(gb_s - gc[:lo, :]), jnp.zeros((CHUNK - lo, HEAD_DIM), F32)],
                axis=0)
            a_rows.append(_dot_nt(q_off, k_off))
        a_mat = a_mat + jnp.concatenate(a_rows, axis=0)

        state_t = s_ref[h]
        o_ref[h, rs, :] = _dot(a_mat, v) + _dot_nt(q * jnp.exp(gc), state_t)
        g_last = gc[CHUNK - 1:CHUNK, :]
        s_ref[h] = state_t * jnp.exp(g_last) + _dot_tn(v, k * jnp.exp(g_last - gc))

    def chunk_body(c, carry):
        def head_body(h, inner):
            head_chunk(h, c)
            return inner
        return lax.fori_loop(0, HEADS, head_body, carry)

    lax.fori_loop(0, n_chunks, chunk_body, 0)


def _sub_block_column_selector():
    j = np.arange(SUB * HEAD_DIM) // HEAD_DIM
    c = np.arange(CHUNK) % SUB
    return jnp.asarray(j[:, None] == c[None, :], dtype=BF16)


def _hgrn_rec(q, k, v, g, batch, seq_len):
    rows = ROWS_REC
    spec = pl.BlockSpec((HEADS, rows, HEAD_DIM),
                        lambda b, t: (0, b * (seq_len // rows) + t, 0))
    return pl.pallas_call(
        functools.partial(_hgrn_rec_kernel, n_chunks=rows // CHUNK),
        grid=(batch, seq_len // rows),
        in_specs=[spec] * 4 + [_resident((SUB * HEAD_DIM, CHUNK))],
        out_specs=spec,
        out_shape=jax.ShapeDtypeStruct(q.shape, F32),
        scratch_shapes=[pltpu.VMEM((HEADS, HEAD_DIM, HEAD_DIM), F32)],
        compiler_params=_params(2),
        name="hgrn_rec",
    )(q, k, v, g, _sub_block_column_selector())


def _out_mlp_kernel(o_ref, sg_ref, h_ref, mw_ref, wout_ref, nmlp_ref, wup_ref, wdown_ref,
                    nfin_ref, out_ref, *, per_head_norm, final_norm):
    if per_head_norm:
        y = jnp.concatenate([_rms(o_ref[h], mw_ref[...]) for h in range(HEADS)], axis=1)
    else:
        y = _rms(jnp.concatenate([o_ref[h] for h in range(HEADS)], axis=1), mw_ref[...])
    y = y * sg_ref[...]
    h1 = h_ref[...] + jnp.dot(y.astype(BF16), wout_ref[...], preferred_element_type=F32)
    xn = _rms(h1, nmlp_ref[...]).astype(BF16)
    acc = h1
    for c in range(MLP_HIDDEN // D_MODEL):
        cs = slice(c * D_MODEL, (c + 1) * D_MODEL)
        up = jnp.maximum(jnp.dot(xn, wup_ref[:, cs], preferred_element_type=F32), 0.0)
        acc = acc + jnp.dot((up * up).astype(BF16), wdown_ref[cs, :], preferred_element_type=F32)
    if final_norm:
        acc = _rms(acc, nfin_ref[...])
    out_ref[...] = acc


def _out_mlp(o, sg, h, mix_norm_w, w_out, norm_mlp, w_up, w_down, norm_final,
             per_head_norm, final_norm):
    n = h.shape[0]
    rows = ROWS_OUT
    row_spec = pl.BlockSpec((rows, D_MODEL), lambda i: (i, 0))
    mw = mix_norm_w.reshape(1, -1).astype(F32)
    return pl.pallas_call(
        functools.partial(_out_mlp_kernel, per_head_norm=per_head_norm, final_norm=final_norm),
        grid=(n // rows,),
        in_specs=[pl.BlockSpec((HEADS, rows, HEAD_DIM), lambda i: (0, i, 0)), row_spec, row_spec,
                  _resident(mw.shape), _resident((D_MODEL, D_MODEL)), _resident((1, D_MODEL)),
                  _resident((D_MODEL, MLP_HIDDEN)), _resident((MLP_HIDDEN, D_MODEL)),
                  _resident((1, D_MODEL))],
        out_specs=row_spec,
        out_shape=jax.ShapeDtypeStruct((n, D_MODEL), F32),
        compiler_params=_params(1),
        name="out_mlp",
    )(o, sg, h, mw, w_out.astype(BF16), norm_mlp.reshape(1, D_MODEL), w_up.astype(BF16),
      w_down.astype(BF16), norm_final.reshape(1, D_MODEL))


def kernel(x, gdn_w_in, gdn_conv, gdn_a_log, gdn_dt_bias, gdn_onorm, gdn_w_out, hgrn_w_in, hgrn_lb_logits, hgrn_gnorm, hgrn_w_out, norm_mix, norm_mlp, mlp_w_up, mlp_w_down, norm_final):
    batch, seq_len, d_model = x.shape
    depth = norm_mix.shape[0]
    assert d_model == D_MODEL and seq_len % max(ROWS_IN, ROWS_REC, ROWS_OUT) == 0
    h = x.reshape(batch * seq_len, d_model)
    for i in range(depth):
        j = i // 2
        if i % 2 == 0:
            q, k, v, sg, g, beta = _gdn_in(h, norm_mix[i], gdn_w_in[j], gdn_conv[j],
                                           gdn_a_log[j], gdn_dt_bias[j], seq_len)
            o = _gdn_rec(q, k, v, g, beta, batch, seq_len)
            mix_norm_w, w_out = gdn_onorm[j], gdn_w_out[j]
        else:
            q, k, v, g, sg = _hgrn_in(h, norm_mix[i], hgrn_w_in[j], hgrn_lb_logits, i)
            o = _hgrn_rec(q, k, v, g, batch, seq_len)
            mix_norm_w, w_out = hgrn_gnorm[j], hgrn_w_out[j]
        h = _out_mlp(o, sg, h, mix_norm_w, w_out, norm_mlp[i], mlp_w_up[i], mlp_w_down[i],
                     norm_final, per_head_norm=(i % 2 == 0), final_norm=(i == depth - 1))
    return h.reshape(batch, seq_len, d_model)
```

```python
import functools

import numpy as np
import jax
import jax.numpy as jnp
from jax import lax
from jax.experimental import pallas as pl
from jax.experimental.pallas import tpu as pltpu

F32 = jnp.float32
BF16 = jnp.bfloat16

D_MODEL = 1024
HEADS = 8
HEAD_DIM = 128
CHUNK = 64
SUB = 16
N_SUB = CHUNK // SUB
CONV_K = 4
MLP_HIDDEN = 4 * D_MODEL
EPS = 1e-6
LANES = 128
HALO_ROWS = 8
GDN_PROJ_WIDTH = 4 * D_MODEL + LANES
V7X_VMEM_LIMIT_BYTES = 56 * 1024 * 1024

ROWS_IN = 256
ROWS_REC = 256
ROWS_OUT = 256


def _resident(shape):
    nd = len(shape)
    return pl.BlockSpec(shape, lambda *_: (0,) * nd, pipeline_mode=pl.Buffered(1))


def _params(n_axes):
    return pltpu.CompilerParams(
        dimension_semantics=("arbitrary",) * n_axes,
        vmem_limit_bytes=V7X_VMEM_LIMIT_BYTES)


def _dot(a, b):
    return jnp.dot(a.astype(BF16), b.astype(BF16), preferred_element_type=F32)


def _dot_nt(a, b):
    return lax.dot_general(a.astype(BF16), b.astype(BF16),
                           (((1,), (1,)), ((), ())), preferred_element_type=F32)


def _dot_tn(a, b):
    return lax.dot_general(a.astype(BF16), b.astype(BF16),
                           (((0,), (0,)), ((), ())), preferred_element_type=F32)


def _split2(x):
    hi = x.astype(BF16)
    lo = (x - hi.astype(F32)).astype(BF16)
    return hi, lo


def _dot_split(a_split, b_split):
    a_hi, a_lo = a_split
    b_hi, b_lo = b_split
    d = functools.partial(jnp.dot, preferred_element_type=F32)
    return (d(jnp.concatenate([a_hi, a_lo], axis=1), jnp.concatenate([b_hi, b_hi], axis=0))
            + d(a_hi, b_lo))


def _tril_consts():
    row = lax.broadcasted_iota(jnp.int32, (CHUNK, 2 * CHUNK), 0)
    col = lax.broadcasted_iota(jnp.int32, (CHUNK, 2 * CHUNK), 1)
    tril2 = jnp.where(row >= (col & (CHUNK - 1)), 1.0, 0.0).astype(BF16)
    return tril2[:, :CHUNK], tril2


def _cumsum_heads(tril, tril2, tiles):
    x = jnp.concatenate(tiles, axis=1)
    x1 = x.astype(BF16)
    r1 = x - x1.astype(F32)
    x2 = r1.astype(BF16)
    x3 = (r1 - x2.astype(F32)).astype(BF16)
    d = functools.partial(jnp.dot, preferred_element_type=F32)
    out = d(tril2, jnp.concatenate([x1, x2], axis=0)) + d(tril, x3)
    return [out[:, h * LANES:(h + 1) * LANES] for h in range(len(tiles))]


def _rms(x, w):
    ms = jnp.mean(x * x, axis=-1, keepdims=True)
    return x * lax.rsqrt(ms + EPS) * w


def _sigmoid(x):
    return 1.0 / (1.0 + jnp.exp(-x))


def _silu(x):
    return x * _sigmoid(x)


def _softplus(x):
    return jnp.maximum(x, 0.0) + jnp.log1p(jnp.exp(-jnp.abs(x)))


def _gdn_in_kernel(h_ref, nw_ref, w_ref, cw_ref, alog_ref, dtb_ref,
                   q_ref, k_ref, v_ref, sg_ref, g_ref, b_ref,
                   pbuf, halo, *, rows, tiles_per_seq):
    @pl.when(pl.program_id(0) % tiles_per_seq == 0)
    def _():
        halo[...] = jnp.zeros_like(halo)

    xn = _rms(h_ref[...], nw_ref[...]).astype(BF16)

    for ci, out_ref in enumerate((q_ref, k_ref, v_ref)):
        c0 = ci * D_MODEL
        p = jnp.dot(xn, w_ref[:, c0:c0 + D_MODEL], preferred_element_type=F32)
        pbuf[0:HALO_ROWS, :] = halo[:, c0:c0 + D_MODEL]
        pbuf[HALO_ROWS:HALO_ROWS + rows, :] = p
        halo[:, c0:c0 + D_MODEL] = p[rows - HALO_ROWS:rows, :]
        acc = cw_ref[0:1, c0:c0 + D_MODEL] * pbuf[HALO_ROWS - 3:HALO_ROWS - 3 + rows, :]
        for tap in range(1, CONV_K - 1):
            start = HALO_ROWS - (CONV_K - 1) + tap
            acc = acc + cw_ref[tap:tap + 1, c0:c0 + D_MODEL] * pbuf[start:start + rows, :]
        acc = acc + cw_ref[CONV_K - 1:CONV_K, c0:c0 + D_MODEL] * p
        s = _silu(acc)
        for h in range(HEADS):
            sl = s[:, h * HEAD_DIM:(h + 1) * HEAD_DIM]
            if ci < 2:
                ss = jnp.sum(sl * sl, axis=-1, keepdims=True)
                sl = sl * lax.rsqrt(ss + EPS)
                if ci == 0:
                    sl = sl * (HEAD_DIM ** -0.5)
            out_ref[h] = sl

    pg = jnp.dot(xn, w_ref[:, 3 * D_MODEL:4 * D_MODEL], preferred_element_type=F32)
    sg_ref[...] = _silu(pg)

    pab = jnp.dot(xn, w_ref[:, 4 * D_MODEL:GDN_PROJ_WIDTH], preferred_element_type=F32)
    gv = -jnp.exp(alog_ref[...]) * _softplus(pab + dtb_ref[...])
    bv = _sigmoid(pab)
    for h in range(HEADS):
        g_ref[h] = jnp.broadcast_to(gv[:, h:h + 1], (rows, LANES))
        b_ref[h] = jnp.broadcast_to(bv[:, HEADS + h:HEADS + h + 1], (rows, LANES))


def _gdn_in(h, norm_w, w_in, conv_w, a_log, dt_bias, seq_len):
    n = h.shape[0]
    rows = ROWS_IN
    w = jnp.pad(w_in, ((0, 0), (0, GDN_PROJ_WIDTH - w_in.shape[1]))).astype(BF16)
    alog = jnp.pad(a_log.astype(F32), (0, LANES - HEADS)).reshape(1, LANES)
    dtb = jnp.pad(dt_bias.astype(F32), (0, LANES - HEADS)).reshape(1, LANES)
    head_major = jax.ShapeDtypeStruct((HEADS, n, HEAD_DIM), F32)
    hm_spec = pl.BlockSpec((HEADS, rows, HEAD_DIM), lambda i: (0, i, 0))
    row_spec = pl.BlockSpec((rows, D_MODEL), lambda i: (i, 0))
    return pl.pallas_call(
        functools.partial(_gdn_in_kernel, rows=rows, tiles_per_seq=seq_len // rows),
        grid=(n // rows,),
        in_specs=[row_spec, _resident((1, D_MODEL)), _resident((D_MODEL, GDN_PROJ_WIDTH)),
                  _resident((CONV_K, 3 * D_MODEL)), _resident((1, LANES)), _resident((1, LANES))],
        out_specs=[hm_spec, hm_spec, hm_spec, row_spec, hm_spec, hm_spec],
        out_shape=[head_major, head_major, head_major,
                   jax.ShapeDtypeStruct((n, D_MODEL), F32), head_major, head_major],
        scratch_shapes=[pltpu.VMEM((HALO_ROWS + rows, D_MODEL), F32),
                        pltpu.VMEM((HALO_ROWS, 3 * D_MODEL), F32)],
        compiler_params=_params(1),
        name="gdn_in",
    )(h, norm_w.reshape(1, D_MODEL), w, conv_w, alog, dtb)


def _gdn_rec_kernel(q_ref, k_ref, v_ref, g_ref, b_ref, o_ref, s_ref, *, n_chunks):
    @pl.when(pl.program_id(1) == 0)
    def _():
        s_ref[...] = jnp.zeros_like(s_ref)

    pair = 2 * CHUNK
    n_pairs = HEADS // 2
    tril, tril2 = _tril_consts()
    row = lax.broadcasted_iota(jnp.int32, (pair, pair), 0)
    col = lax.broadcasted_iota(jnp.int32, (pair, pair), 1)
    first_head = row < CHUNK
    same_head = first_head == (col < CHUNK)
    causal = same_head & (row >= col)
    strict = same_head & (row > col)
    eye = jnp.where(row == col, 1.0, 0.0).astype(F32)

    def halves(x):
        return x[:CHUNK], x[CHUNK:]

    def body(c, carry):
        rs = pl.ds(pl.multiple_of(c * CHUNK, CHUNK), CHUNK)

        def load_pairs(ref):
            return [jnp.concatenate([ref[2 * p, rs, :], ref[2 * p + 1, rs, :]], axis=0)
                    for p in range(n_pairs)]

        q, k, v, beta = (load_pairs(r) for r in (q_ref, k_ref, v_ref, b_ref))
        gc_h = _cumsum_heads(tril, tril2, [g_ref[h, rs, :] for h in range(HEADS)])
        gc = [jnp.concatenate([gc_h[2 * p], gc_h[2 * p + 1]], axis=0) for p in range(n_pairs)]
        decay = [jnp.exp(jnp.where(causal, x - x.T, -jnp.inf)) for x in gc]
        kb = [a * b for a, b in zip(k, beta)]
        lmat = [jnp.where(strict, _dot_nt(a, b) * d, 0.0) for a, b, d in zip(kb, k, decay)]
        a_qk = [jnp.where(causal, _dot_nt(a, b) * d, 0.0) for a, b, d in zip(q, k, decay)]
        egc = [jnp.exp(x) for x in gc]
        rhs = [jnp.concatenate([a * b, kb_ * e], axis=1)
               for a, b, kb_, e in zip(v, beta, kb, egc)]
        pw_s = [_split2(-x) for x in lmat]
        inv = [eye - x for x in lmat]
        for _ in range(5):
            pw = [_dot_split(s, s) for s in pw_s]
            pw_s = [_split2(x) for x in pw]
            inv = [i + _dot_split(_split2(i), s) for i, s in zip(inv, pw_s)]
        sol = [_dot_split(_split2(i), _split2(r)) for i, r in zip(inv, rhs)]
        u = [x[:, :HEAD_DIM] for x in sol]
        w = [x[:, HEAD_DIM:] for x in sol]

        state = [s_ref[h] for h in range(HEADS)]
        w_s = [_dot(a, s) for a, s in zip([t for x in w for t in halves(x)], state)]
        v_new = [a - jnp.concatenate([w_s[2 * p], w_s[2 * p + 1]], axis=0)
                 for p, a in enumerate(u)]
        qe_h = [t for a, e in zip(q, egc) for t in halves(a * e)]
        q_s = [_dot(a, s) for a, s in zip(qe_h, state)]
        o = [jnp.concatenate([q_s[2 * p], q_s[2 * p + 1]], axis=0) + _dot(a, b)
             for p, (a, b) in enumerate(zip(a_qk, v_new))]
        for p in range(n_pairs):
            o_ref[2 * p, rs, :] = o[p][:CHUNK]
            o_ref[2 * p + 1, rs, :] = o[p][CHUNK:]
        g_last = [jnp.where(first_head, x[CHUNK - 1:CHUNK, :], x[pair - 1:pair, :]) for x in gc]
        k_dec = [a * jnp.exp(gl - x) for a, gl, x in zip(k, g_last, gc)]
        upd = [_dot_tn(a, b) for a, b in zip([t for x in k_dec for t in halves(x)],
                                             [t for x in v_new for t in halves(x)])]
        for h in range(HEADS):
            s_ref[h] = state[h] * jnp.exp(gc_h[h][CHUNK - 1:CHUNK, :]) + upd[h]
        return carry

    lax.fori_loop(0, n_chunks, body, 0)


def _gdn_rec(q, k, v, g, beta, batch, seq_len):
    rows = ROWS_REC
    spec = pl.BlockSpec((HEADS, rows, HEAD_DIM),
                        lambda b, t: (0, b * (seq_len // rows) + t, 0))
    return pl.pallas_call(
        functools.partial(_gdn_rec_kernel, n_chunks=rows // CHUNK),
        grid=(batch, seq_len // rows),
        in_specs=[spec] * 5,
        out_specs=spec,
        out_shape=jax.ShapeDtypeStruct(q.shape, F32),
        scratch_shapes=[pltpu.VMEM((HEADS, HEAD_DIM, HEAD_DIM), F32)],
        compiler_params=_params(2),
        name="gdn_rec",
    )(q, k, v, g, beta)


def _hgrn_in_kernel(h_ref, nw_ref, w_ref, lg_ref,
                    q_ref, k_ref, v_ref, g_ref, sg_ref, *, layer):
    lg = lg_ref[...]
    m = jnp.max(lg, axis=0, keepdims=True)
    e = jnp.exp(lg - m)
    sm = e / jnp.sum(e, axis=0, keepdims=True)
    cs = sm[0:1, :]
    for r in range(1, layer + 1):
        cs = cs + sm[r:r + 1, :]
    lb = cs - sm[0:1, :]
    log_lb = jnp.log(lb)
    log_1m_lb = jnp.log1p(-lb)

    xn = _rms(h_ref[...], nw_ref[...]).astype(BF16)

    def proj(i):
        return jnp.dot(xn, w_ref[:, i * D_MODEL:(i + 1) * D_MODEL], preferred_element_type=F32)

    def heads_out(ref, val):
        for h in range(HEADS):
            ref[h] = val[:, h * HEAD_DIM:(h + 1) * HEAD_DIM]

    heads_out(q_ref, _silu(proj(0)) * (HEAD_DIM ** -0.5))

    f = proj(1)
    t = jnp.exp(-jnp.abs(f))
    log_sig = jnp.minimum(f, 0.0) - jnp.log1p(t)
    b = log_1m_lb + log_sig
    hi = jnp.maximum(log_lb, b)
    heads_out(g_ref, hi + jnp.log1p(jnp.exp(-jnp.abs(log_lb - b))))
    inv = 1.0 / (1.0 + t)
    heads_out(k_ref, (1.0 - lb) * jnp.where(f >= 0.0, t * inv, inv))

    heads_out(v_ref, proj(2))
    sg_ref[...] = _silu(proj(3))


def _hgrn_in(h, norm_w, w_in, lb_logits, layer):
    n = h.shape[0]
    rows = ROWS_IN
    depth = lb_logits.shape[0]
    head_major = jax.ShapeDtypeStruct((HEADS, n, HEAD_DIM), F32)
    hm_spec = pl.BlockSpec((HEADS, rows, HEAD_DIM), lambda i: (0, i, 0))
    row_spec = pl.BlockSpec((rows, D_MODEL), lambda i: (i, 0))
    return pl.pallas_call(
        functools.partial(_hgrn_in_kernel, layer=layer),
        grid=(n // rows,),
        in_specs=[row_spec, _resident((1, D_MODEL)), _resident((D_MODEL, 4 * D_MODEL)),
                  _resident((depth, D_MODEL))],
        out_specs=[hm_spec, hm_spec, hm_spec, hm_spec, row_spec],
        out_shape=[head_major] * 4 + [jax.ShapeDtypeStruct((n, D_MODEL), F32)],
        compiler_params=_params(1),
        name="hgrn_in",
    )(h, norm_w.reshape(1, D_MODEL), w_in.astype(BF16), lb_logits.astype(F32))


def _hgrn_rec_kernel(q_ref, k_ref, v_ref, g_ref, ecol_ref, o_ref, s_ref, *, n_chunks):
    @pl.when(pl.program_id(1) == 0)
    def _():
        s_ref[...] = jnp.zeros_like(s_ref)

    tril, tril2 = _tril_consts()
    row = lax.broadcasted_iota(jnp.int32, (CHUNK, CHUNK), 0)
    col = lax.broadcasted_iota(jnp.int32, (CHUNK, CHUNK), 1)
    same_sub = (row // SUB) == (col // SUB)
    sub_i = lax.broadcasted_iota(jnp.int32, (N_SUB, SUB, HEAD_DIM), 1)
    heads = range(HEADS)

    def diag_products(q, k, gc):
        g3 = gc.reshape(N_SUB, SUB, HEAD_DIM)
        q3 = q.reshape(N_SUB, SUB, HEAD_DIM)
        k3 = k.reshape(N_SUB, SUB, HEAD_DIM)
        pieces = []
        for j in range(SUB):
            dec = jnp.exp(jnp.where(sub_i >= j, g3 - g3[:, j:j + 1, :], -jnp.inf))
            pieces.append((q3 * dec * k3[:, j:j + 1, :]).reshape(CHUNK, HEAD_DIM).astype(BF16))
        return jnp.concatenate(pieces, axis=1)

    def chunk_body(c, carry):
        rs = pl.ds(pl.multiple_of(c * CHUNK, CHUNK), CHUNK)
        q = [q_ref[h, rs, :] for h in heads]
        k = [k_ref[h, rs, :] for h in heads]
        v = [v_ref[h, rs, :] for h in heads]
        g = [g_ref[h, rs, :] for h in heads]
        gc = _cumsum_heads(tril, tril2, g)

        prods = jnp.concatenate([diag_products(a, b, x) for a, b, x in zip(q, k, gc)], axis=0)
        a_all = jnp.dot(prods, ecol_ref[...], preferred_element_type=F32)
        a_mat = [jnp.where(same_sub, a_all[h * CHUNK:(h + 1) * CHUNK], 0.0) for h in heads]

        gb = [x - y for x, y in zip(gc, g)]
        a_rows = [[jnp.zeros((SUB, CHUNK), F32)] for _ in heads]
        for s in range(1, N_SUB):
            lo, hi = s * SUB, (s + 1) * SUB
            for h in heads:
                gb_s = gb[h][lo:lo + 1, :]
                q_off = q[h][lo:hi, :] * jnp.exp(gc[h][lo:hi, :] - gb_s)
                k_off = jnp.concatenate(
                    [k[h][:lo, :] * jnp.exp(gb_s - gc[h][:lo, :]),
                     jnp.zeros((CHUNK - lo, HEAD_DIM), F32)], axis=0)
                a_rows[h].append(_dot_nt(q_off, k_off))
        a_mat = [a + jnp.concatenate(r, axis=0) for a, r in zip(a_mat, a_rows)]

        state_t = [s_ref[h] for h in heads]
        o_intra = [_dot(a, b) for a, b in zip(a_mat, v)]
        o_state = [_dot_nt(a * jnp.exp(x), s) for a, x, s in zip(q, gc, state_t)]
        for h in heads:
            o_ref[h, rs, :] = o_intra[h] + o_state[h]
        g_last = [x[CHUNK - 1:CHUNK, :] for x in gc]
        upd = [_dot_tn(a, b * jnp.exp(gl - x)) for a, b, gl, x in zip(v, k, g_last, gc)]
        for h in heads:
            s_ref[h] = state_t[h] * jnp.exp(g_last[h]) + upd[h]
        return carry

    lax.fori_loop(0, n_chunks, chunk_body, 0)


def _sub_block_column_selector():
    j = np.arange(SUB * HEAD_DIM) // HEAD_DIM
    c = np.arange(CHUNK) % SUB
    return jnp.asarray(j[:, None] == c[None, :], dtype=BF16)


def _hgrn_rec(q, k, v, g, batch, seq_len):
    rows = ROWS_REC
    spec = pl.BlockSpec((HEADS, rows, HEAD_DIM),
                        lambda b, t: (0, b * (seq_len // rows) + t, 0))
    return pl.pallas_call(
        functools.partial(_hgrn_rec_kernel, n_chunks=rows // CHUNK),
        grid=(batch, seq_len // rows),
        in_specs=[spec] * 4 + [_resident((SUB * HEAD_DIM, CHUNK))],
        out_specs=spec,
        out_shape=jax.ShapeDtypeStruct(q.shape, F32),
        scratch_shapes=[pltpu.VMEM((HEADS, HEAD_DIM, HEAD_DIM), F32)],
        compiler_params=_params(2),
        name="hgrn_rec",
    )(q, k, v, g, _sub_block_column_selector())


def _out_mlp_kernel(o_ref, sg_ref, h_ref, mw_ref, wout_ref, nmlp_ref, wup_ref, wdown_ref,
                    nfin_ref, out_ref, *, per_head_norm, final_norm):
    if per_head_norm:
        y = jnp.concatenate([_rms(o_ref[h], mw_ref[...]) for h in range(HEADS)], axis=1)
    else:
        y = _rms(jnp.concatenate([o_ref[h] for h in range(HEADS)], axis=1), mw_ref[...])
    y = y * sg_ref[...]
    h1 = h_ref[...] + jnp.dot(y.astype(BF16), wout_ref[...], preferred_element_type=F32)
    xn = _rms(h1, nmlp_ref[...]).astype(BF16)
    acc = h1
    for c in range(MLP_HIDDEN // D_MODEL):
        cs = slice(c * D_MODEL, (c + 1) * D_MODEL)
        up = jnp.maximum(jnp.dot(xn, wup_ref[:, cs], preferred_element_type=F32), 0.0)
        acc = acc + jnp.dot((up * up).astype(BF16), wdown_ref[cs, :], preferred_element_type=F32)
    if final_norm:
        acc = _rms(acc, nfin_ref[...])
    out_ref[...] = acc


def _out_mlp(o, sg, h, mix_norm_w, w_out, norm_mlp, w_up, w_down, norm_final,
             per_head_norm, final_norm):
    n = h.shape[0]
    rows = ROWS_OUT
    row_spec = pl.BlockSpec((rows, D_MODEL), lambda i: (i, 0))
    mw = mix_norm_w.reshape(1, -1).astype(F32)
    return pl.pallas_call(
        functools.partial(_out_mlp_kernel, per_head_norm=per_head_norm, final_norm=final_norm),
        grid=(n // rows,),
        in_specs=[pl.BlockSpec((HEADS, rows, HEAD_DIM), lambda i: (0, i, 0)), row_spec, row_spec,
                  _resident(mw.shape), _resident((D_MODEL, D_MODEL)), _resident((1, D_MODEL)),
                  _resident((D_MODEL, MLP_HIDDEN)), _resident((MLP_HIDDEN, D_MODEL)),
                  _resident((1, D_MODEL))],
        out_specs=row_spec,
        out_shape=jax.ShapeDtypeStruct((n, D_MODEL), F32),
        compiler_params=_params(1),
        name="out_mlp",
    )(o, sg, h, mw, w_out.astype(BF16), norm_mlp.reshape(1, D_MODEL), w_up.astype(BF16),
      w_down.astype(BF16), norm_final.reshape(1, D_MODEL))


def kernel(x, gdn_w_in, gdn_conv, gdn_a_log, gdn_dt_bias, gdn_onorm, gdn_w_out, hgrn_w_in, hgrn_lb_logits, hgrn_gnorm, hgrn_w_out, norm_mix, norm_mlp, mlp_w_up, mlp_w_down, norm_final):
    batch, seq_len, d_model = x.shape
    depth = norm_mix.shape[0]
    assert d_model == D_MODEL and seq_len % max(ROWS_IN, ROWS_REC, ROWS_OUT) == 0
    h = x.reshape(batch * seq_len, d_model)
    for i in range(depth):
        j = i // 2
        if i % 2 == 0:
            q, k, v, sg, g, beta = _gdn_in(h, norm_mix[i], gdn_w_in[j], gdn_conv[j],
                                           gdn_a_log[j], gdn_dt_bias[j], seq_len)
            o = _gdn_rec(q, k, v, g, beta, batch, seq_len)
            mix_norm_w, w_out = gdn_onorm[j], gdn_w_out[j]
        else:
            q, k, v, g, sg = _hgrn_in(h, norm_mix[i], hgrn_w_in[j], hgrn_lb_logits, i)
            o = _hgrn_rec(q, k, v, g, batch, seq_len)
            mix_norm_w, w_out = hgrn_gnorm[j], hgrn_w_out[j]
        h = _out_mlp(o, sg, h, mix_norm_w, w_out, norm_mlp[i], mlp_w_up[i], mlp_w_down[i],
                     norm_final, per_head_norm=(i % 2 == 0), final_norm=(i == depth - 1))
    return h.reshape(batch, seq_len, d_model)
```

```python
import functools

import numpy as np
import jax
import jax.numpy as jnp
from jax import lax
from jax.experimental import pallas as pl
from jax.experimental.pallas import tpu as pltpu

F32 = jnp.float32
BF16 = jnp.bfloat16

D_MODEL = 1024
HEADS = 8
HEAD_DIM = 128
CHUNK = 64
SUB = 16
N_SUB = CHUNK // SUB
CONV_K = 4
MLP_HIDDEN = 4 * D_MODEL
EPS = 1e-6
LOG2_E = 1.4426950408889634
LANES = 128
HALO_ROWS = 8
COL_GROUP = 256
HEADS_PER_GROUP = COL_GROUP // HEAD_DIM
GDN_PROJ_WIDTH = 4 * D_MODEL + LANES
V7X_VMEM_LIMIT_BYTES = 56 * 1024 * 1024

ROWS_IN = 256
ROWS_REC = 256
ROWS_OUT = 256


def _resident(shape):
    nd = len(shape)
    return pl.BlockSpec(shape, lambda *_: (0,) * nd, pipeline_mode=pl.Buffered(1))


def _params(n_axes):
    return pltpu.CompilerParams(
        dimension_semantics=("arbitrary",) * n_axes,
        vmem_limit_bytes=V7X_VMEM_LIMIT_BYTES)


def _dot(a, b):
    return jnp.dot(a.astype(BF16), b.astype(BF16), preferred_element_type=F32)


def _dot_nt(a, b):
    return lax.dot_general(a.astype(BF16), b.astype(BF16),
                           (((1,), (1,)), ((), ())), preferred_element_type=F32)


def _dot_tn(a, b):
    return lax.dot_general(a.astype(BF16), b.astype(BF16),
                           (((0,), (0,)), ((), ())), preferred_element_type=F32)


def _split2(x):
    hi = x.astype(BF16)
    lo = (x - hi.astype(F32)).astype(BF16)
    return hi, lo


def _dot_split(a_split, b_split):
    a_hi, a_lo = a_split
    b_hi, b_lo = b_split
    d = functools.partial(jnp.dot, preferred_element_type=F32)
    return (d(jnp.concatenate([a_hi, a_lo], axis=1), jnp.concatenate([b_hi, b_hi], axis=0))
            + d(a_hi, b_lo))


def _tril_consts():
    row = lax.broadcasted_iota(jnp.int32, (CHUNK, 2 * CHUNK), 0)
    col = lax.broadcasted_iota(jnp.int32, (CHUNK, 2 * CHUNK), 1)
    tril2 = jnp.where(row >= (col & (CHUNK - 1)), 1.0, 0.0).astype(BF16)
    return tril2[:, :CHUNK], tril2


def _cumsum_heads(tril, tril2, tiles):
    x = jnp.concatenate(tiles, axis=1)
    x1 = x.astype(BF16)
    r1 = x - x1.astype(F32)
    x2 = r1.astype(BF16)
    x3 = (r1 - x2.astype(F32)).astype(BF16)
    d = functools.partial(jnp.dot, preferred_element_type=F32)
    out = d(tril2, jnp.concatenate([x1, x2], axis=0)) + d(tril, x3)
    return [out[:, h * LANES:(h + 1) * LANES] for h in range(len(tiles))]


def _rms(x, w):
    ms = jnp.mean(x * x, axis=-1, keepdims=True)
    return x * lax.rsqrt(ms + EPS) * w


def _sigmoid(x):
    return 1.0 / (1.0 + jnp.exp(-x))


def _silu(x, scale=1.0):
    half = x * 0.5
    hs = half if scale == 1.0 else x * (0.5 * scale)
    return hs + hs * jnp.tanh(half)


def _softplus(x):
    return jnp.maximum(x, 0.0) + jnp.log(1.0 + jnp.exp(-jnp.abs(x)))


def _gdn_in_kernel(h_ref, nw_ref, w_ref, cw_ref, alog_ref, dtb_ref,
                   q_ref, k_ref, v_ref, sg_ref, g_ref, b_ref,
                   pbuf, halo, *, rows, tiles_per_seq):
    @pl.when(pl.program_id(0) % tiles_per_seq == 0)
    def _():
        halo[...] = jnp.zeros_like(halo)

    xn = _rms(h_ref[...], nw_ref[...]).astype(BF16)

    for ci, out_ref in enumerate((q_ref, k_ref, v_ref)):
        for grp in range(D_MODEL // COL_GROUP):
            cs = slice(ci * D_MODEL + grp * COL_GROUP, ci * D_MODEL + (grp + 1) * COL_GROUP)
            p = jnp.dot(xn, w_ref[:, cs], preferred_element_type=F32)
            pbuf[0:HALO_ROWS, cs] = halo[:, cs]
            pbuf[HALO_ROWS:HALO_ROWS + rows, cs] = p
            halo[:, cs] = p[rows - HALO_ROWS:rows, :]
            acc = cw_ref[CONV_K - 1:CONV_K, cs] * p
            for tap in range(CONV_K - 1):
                start = HALO_ROWS - (CONV_K - 1) + tap
                acc = acc + cw_ref[tap:tap + 1, cs] * pbuf[start:start + rows, cs]
            s = _silu(acc)
            for hh in range(HEADS_PER_GROUP):
                sl = s[:, hh * HEAD_DIM:(hh + 1) * HEAD_DIM]
                if ci < 2:
                    ss = jnp.sum(sl * sl, axis=-1, keepdims=True)
                    sl = sl * (lax.rsqrt(ss + EPS) * (HEAD_DIM ** -0.5 if ci == 0 else 1.0))
                out_ref[grp * HEADS_PER_GROUP + hh] = sl

    for grp in range(D_MODEL // COL_GROUP):
        cs = slice(grp * COL_GROUP, (grp + 1) * COL_GROUP)
        pg = jnp.dot(xn, w_ref[:, 3 * D_MODEL + cs.start:3 * D_MODEL + cs.stop],
                     preferred_element_type=F32)
        sg_ref[:, cs] = _silu(pg)

    pab = jnp.dot(xn, w_ref[:, 4 * D_MODEL:GDN_PROJ_WIDTH], preferred_element_type=F32)
    gv = -jnp.exp(alog_ref[...]) * _softplus(pab + dtb_ref[...])
    bv = _sigmoid(pab)
    for h in range(HEADS):
        g_ref[h] = jnp.broadcast_to(gv[:, h:h + 1], (rows, LANES))
        b_ref[h] = jnp.broadcast_to(bv[:, HEADS + h:HEADS + h + 1], (rows, LANES))


def _gdn_in(h, norm_w, w_in, conv_w, a_log, dt_bias, seq_len):
    n = h.shape[0]
    rows = ROWS_IN
    w = jnp.pad(w_in, ((0, 0), (0, GDN_PROJ_WIDTH - w_in.shape[1]))).astype(BF16)
    alog = jnp.pad(a_log.astype(F32), (0, LANES - HEADS)).reshape(1, LANES)
    dtb = jnp.pad(dt_bias.astype(F32), (0, LANES - HEADS)).reshape(1, LANES)
    head_major = jax.ShapeDtypeStruct((HEADS, n, HEAD_DIM), F32)
    hm_spec = pl.BlockSpec((HEADS, rows, HEAD_DIM), lambda i: (0, i, 0))
    row_spec = pl.BlockSpec((rows, D_MODEL), lambda i: (i, 0))
    return pl.pallas_call(
        functools.partial(_gdn_in_kernel, rows=rows, tiles_per_seq=seq_len // rows),
        grid=(n // rows,),
        in_specs=[row_spec, _resident((1, D_MODEL)), _resident((D_MODEL, GDN_PROJ_WIDTH)),
                  _resident((CONV_K, 3 * D_MODEL)), _resident((1, LANES)), _resident((1, LANES))],
        out_specs=[hm_spec, hm_spec, hm_spec, row_spec, hm_spec, hm_spec],
        out_shape=[head_major, head_major, head_major,
                   jax.ShapeDtypeStruct((n, D_MODEL), F32), head_major, head_major],
        scratch_shapes=[pltpu.VMEM((HALO_ROWS + rows, 3 * D_MODEL), F32),
                        pltpu.VMEM((HALO_ROWS, 3 * D_MODEL), F32)],
        compiler_params=_params(1),
        name="gdn_in",
    )(h, norm_w.reshape(1, D_MODEL), w, conv_w, alog, dtb)


def _gdn_rec_kernel(q_ref, k_ref, v_ref, g_ref, b_ref, o_ref, s_ref, *, n_chunks):
    @pl.when(pl.program_id(1) == 0)
    def _():
        s_ref[...] = jnp.zeros_like(s_ref)

    pair = 2 * CHUNK
    n_pairs = HEADS // 2
    tril, tril2 = _tril_consts()
    row = lax.broadcasted_iota(jnp.int32, (pair, pair), 0)
    col = lax.broadcasted_iota(jnp.int32, (pair, pair), 1)
    first_head = row < CHUNK
    same_head = first_head == (col < CHUNK)
    causal = same_head & (row >= col)
    strict = same_head & (row > col)
    eye = jnp.where(row == col, 1.0, 0.0).astype(F32)

    def halves(x):
        return x[:CHUNK], x[CHUNK:]

    items = [(c, p) for c in range(n_chunks) for p in range(n_pairs)]

    def load_pairs(ref):
        return [jnp.concatenate([ref[2 * p, c * CHUNK:(c + 1) * CHUNK, :],
                                 ref[2 * p + 1, c * CHUNK:(c + 1) * CHUNK, :]], axis=0)
                for c, p in items]

    q, k, v, beta = (load_pairs(r) for r in (q_ref, k_ref, v_ref, b_ref))
    gc_h = _cumsum_heads(tril, tril2, [g_ref[h, c * CHUNK:(c + 1) * CHUNK, :]
                                       for c in range(n_chunks) for h in range(HEADS)])
    gc = [jnp.concatenate([gc_h[c * HEADS + 2 * p], gc_h[c * HEADS + 2 * p + 1]], axis=0)
          for c, p in items]
    decay = [jnp.exp(jnp.where(causal, x - x.T, -jnp.inf)) for x in gc]
    kb = [a * b for a, b in zip(k, beta)]
    lmat = [jnp.where(strict, _dot_nt(a, b) * d, 0.0) for a, b, d in zip(kb, k, decay)]
    a_qk = [jnp.where(causal, _dot_nt(a, b) * d, 0.0) for a, b, d in zip(q, k, decay)]
    egc = [jnp.exp(x) for x in gc]
    rhs = [jnp.concatenate([a * b, kb_ * e], axis=1) for a, b, kb_, e in zip(v, beta, kb, egc)]
    pw_s = [_split2(-x) for x in lmat]
    inv = [eye - x for x in lmat]
    for _ in range(5):
        pw = [_dot_split(s, s) for s in pw_s]
        pw_s = [_split2(x) for x in pw]
        inv = [i + _dot_split(_split2(i), s) for i, s in zip(inv, pw_s)]
    sol = [_dot_split(_split2(i), _split2(r)) for i, r in zip(inv, rhs)]
    qe = [a * e for a, e in zip(q, egc)]
    g_last = [jnp.where(first_head, x[CHUNK - 1:CHUNK, :], x[pair - 1:pair, :]) for x in gc]
    k_dec = [a * jnp.exp(gl - x) for a, gl, x in zip(k, g_last, gc)]

    state = [s_ref[h] for h in range(HEADS)]
    for c in range(n_chunks):
        ps = range(c * n_pairs, (c + 1) * n_pairs)
        w_s = [_dot(a, s) for a, s in zip([t for i in ps for t in halves(sol[i][:, HEAD_DIM:])],
                                          state)]
        v_new = [sol[i][:, :HEAD_DIM] - jnp.concatenate([w_s[2 * p], w_s[2 * p + 1]], axis=0)
                 for p, i in enumerate(ps)]
        q_s = [_dot(a, s) for a, s in zip([t for i in ps for t in halves(qe[i])], state)]
        o = [jnp.concatenate([q_s[2 * p], q_s[2 * p + 1]], axis=0) + _dot(a_qk[i], v_new[p])
             for p, i in enumerate(ps)]
        for p in range(n_pairs):
            o_ref[2 * p, c * CHUNK:(c + 1) * CHUNK, :] = o[p][:CHUNK]
            o_ref[2 * p + 1, c * CHUNK:(c + 1) * CHUNK, :] = o[p][CHUNK:]
        upd = [_dot_tn(a, b) for a, b in zip([t for i in ps for t in halves(k_dec[i])],
                                             [t for x in v_new for t in halves(x)])]
        state = [s * jnp.exp(gc_h[c * HEADS + h][CHUNK - 1:CHUNK, :]) + d
                 for h, (s, d) in enumerate(zip(state, upd))]
    for h in range(HEADS):
        s_ref[h] = state[h]


def _gdn_rec(q, k, v, g, beta, batch, seq_len):
    rows = ROWS_REC
    spec = pl.BlockSpec((HEADS, rows, HEAD_DIM),
                        lambda b, t: (0, b * (seq_len // rows) + t, 0))
    return pl.pallas_call(
        functools.partial(_gdn_rec_kernel, n_chunks=rows // CHUNK),
        grid=(batch, seq_len // rows),
        in_specs=[spec] * 5,
        out_specs=spec,
        out_shape=jax.ShapeDtypeStruct(q.shape, F32),
        scratch_shapes=[pltpu.VMEM((HEADS, HEAD_DIM, HEAD_DIM), F32)],
        compiler_params=_params(2),
        name="gdn_rec",
    )(q, k, v, g, beta)


def _hgrn_in_kernel(h_ref, nw_ref, w_ref, lg_ref,
                    q_ref, k_ref, v_ref, g_ref, sg_ref, *, layer):
    lg = lg_ref[...]
    m = jnp.max(lg, axis=0, keepdims=True)
    e = jnp.exp(lg - m)
    sm = e / jnp.sum(e, axis=0, keepdims=True)
    cs = sm[0:1, :]
    for r in range(1, layer + 1):
        cs = cs + sm[r:r + 1, :]
    lb = cs - sm[0:1, :]
    log_lb = jnp.log(lb)
    log_1m_lb = jnp.log1p(-lb)

    xn = _rms(h_ref[...], nw_ref[...]).astype(BF16)

    def proj(i, cs):
        return jnp.dot(xn, w_ref[:, i * D_MODEL + cs.start:i * D_MODEL + cs.stop],
                       preferred_element_type=F32)

    def heads_out(ref, grp, val):
        for hh in range(HEADS_PER_GROUP):
            ref[grp * HEADS_PER_GROUP + hh] = val[:, hh * HEAD_DIM:(hh + 1) * HEAD_DIM]

    for grp in range(D_MODEL // COL_GROUP):
        cs = slice(grp * COL_GROUP, (grp + 1) * COL_GROUP)
        f = proj(1, cs)
        t = jnp.exp(-jnp.abs(f))
        log_sig = jnp.minimum(f, 0.0) - jnp.log(1.0 + t)
        a = log_lb[:, cs]
        b = log_1m_lb[:, cs] + log_sig
        heads_out(g_ref, grp, jnp.maximum(a, b) + jnp.log(1.0 + jnp.exp(-jnp.abs(a - b))))
        inv = 1.0 / (1.0 + t)
        heads_out(k_ref, grp, (1.0 - lb[:, cs]) * jnp.where(f >= 0.0, t * inv, inv))

        heads_out(q_ref, grp, _silu(proj(0, cs), HEAD_DIM ** -0.5))
        sg_ref[:, cs] = _silu(proj(3, cs))
        heads_out(v_ref, grp, proj(2, cs))


def _hgrn_in(h, norm_w, w_in, lb_logits, layer):
    n = h.shape[0]
    rows = ROWS_IN
    depth = lb_logits.shape[0]
    head_major = jax.ShapeDtypeStruct((HEADS, n, HEAD_DIM), F32)
    hm_spec = pl.BlockSpec((HEADS, rows, HEAD_DIM), lambda i: (0, i, 0))
    row_spec = pl.BlockSpec((rows, D_MODEL), lambda i: (i, 0))
    return pl.pallas_call(
        functools.partial(_hgrn_in_kernel, layer=layer),
        grid=(n // rows,),
        in_specs=[row_spec, _resident((1, D_MODEL)), _resident((D_MODEL, 4 * D_MODEL)),
                  _resident((depth, D_MODEL))],
        out_specs=[hm_spec, hm_spec, hm_spec, hm_spec, row_spec],
        out_shape=[head_major] * 4 + [jax.ShapeDtypeStruct((n, D_MODEL), F32)],
        compiler_params=_params(1),
        name="hgrn_in",
    )(h, norm_w.reshape(1, D_MODEL), w_in.astype(BF16), lb_logits.astype(F32))


def _hgrn_rec_kernel(q_ref, k_ref, v_ref, g_ref, ecol_ref, o_ref, s_ref, *, n_chunks):
    @pl.when(pl.program_id(1) == 0)
    def _():
        s_ref[...] = jnp.zeros_like(s_ref)

    tril, tril2 = _tril_consts()
    row = lax.broadcasted_iota(jnp.int32, (CHUNK, CHUNK), 0)
    col = lax.broadcasted_iota(jnp.int32, (CHUNK, CHUNK), 1)
    same_sub = ((row // SUB) == (col // SUB)) & (row >= col)
    sub_i = lax.broadcasted_iota(jnp.int32, (N_SUB, SUB, HEAD_DIM), 1)
    heads = range(HEADS)

    def diag_products(q, k, gc):
        g3 = (gc * LOG2_E).reshape(N_SUB, SUB, HEAD_DIM)
        q3 = q.reshape(N_SUB, SUB, HEAD_DIM)
        k3 = k.reshape(N_SUB, SUB, HEAD_DIM)
        half = SUB // 2
        zeros = jnp.zeros((N_SUB, half, HEAD_DIM), F32)
        pieces = []
        for j in range(SUB):
            lo = 0 if j < half else half
            dec = jnp.exp2(jnp.minimum(g3[:, lo:, :] - g3[:, j:j + 1, :], 0.0))
            piece = q3[:, lo:, :] * dec * k3[:, j:j + 1, :]
            if lo:
                piece = jnp.concatenate([zeros, piece], axis=1)
            pieces.append(piece.reshape(CHUNK, HEAD_DIM).astype(BF16))
        return jnp.concatenate(pieces, axis=1)

    items = [(c, h) for c in range(n_chunks) for h in heads]

    def tiles(ref):
        return [ref[h, c * CHUNK:(c + 1) * CHUNK, :] for c, h in items]

    q, k, v, g = (tiles(r) for r in (q_ref, k_ref, v_ref, g_ref))
    gc = _cumsum_heads(tril, tril2, g)

    a_mat = []
    for c in range(n_chunks):
        ids = range(c * HEADS, (c + 1) * HEADS)
        prods = jnp.concatenate([diag_products(q[i], k[i], gc[i]) for i in ids], axis=0)
        a_all = jnp.dot(prods, ecol_ref[...], preferred_element_type=F32)
        a_mat += [jnp.where(same_sub, a_all[h * CHUNK:(h + 1) * CHUNK], 0.0) for h in heads]

    gb = [x - y for x, y in zip(gc, g)]
    a_rows = [[jnp.zeros((SUB, CHUNK), F32)] for _ in items]
    for s in range(1, N_SUB):
        lo, hi = s * SUB, (s + 1) * SUB
        for i in range(len(items)):
            gb_s = gb[i][lo:lo + 1, :]
            q_off = q[i][lo:hi, :] * jnp.exp(gc[i][lo:hi, :] - gb_s)
            k_off = jnp.concatenate(
                [k[i][:lo, :] * jnp.exp(gb_s - gc[i][:lo, :]),
                 jnp.zeros((CHUNK - lo, HEAD_DIM), F32)], axis=0)
            a_rows[i].append(_dot_nt(q_off, k_off))
    a_mat = [a + jnp.concatenate(r, axis=0) for a, r in zip(a_mat, a_rows)]
    o_intra = [_dot(a, b) for a, b in zip(a_mat, v)]
    qe = [a * jnp.exp(x) for a, x in zip(q, gc)]
    g_last = [x[CHUNK - 1:CHUNK, :] for x in gc]
    upd = [_dot_tn(a, b * jnp.exp(gl - x)) for a, b, gl, x in zip(v, k, g_last, gc)]

    state_t = [s_ref[h] for h in heads]
    for c in range(n_chunks):
        ids = range(c * HEADS, (c + 1) * HEADS)
        o_state = [_dot_nt(qe[i], s) for i, s in zip(ids, state_t)]
        for h, i in enumerate(ids):
            o_ref[h, c * CHUNK:(c + 1) * CHUNK, :] = o_intra[i] + o_state[h]
        state_t = [s * jnp.exp(g_last[i]) + upd[i] for i, s in zip(ids, state_t)]
    for h in heads:
        s_ref[h] = state_t[h]


def _sub_block_column_selector():
    j = np.arange(SUB * HEAD_DIM) // HEAD_DIM
    c = np.arange(CHUNK) % SUB
    return jnp.asarray(j[:, None] == c[None, :], dtype=BF16)


def _hgrn_rec(q, k, v, g, batch, seq_len):
    rows = ROWS_REC
    spec = pl.BlockSpec((HEADS, rows, HEAD_DIM),
                        lambda b, t: (0, b * (seq_len // rows) + t, 0))
    return pl.pallas_call(
        functools.partial(_hgrn_rec_kernel, n_chunks=rows // CHUNK),
        grid=(batch, seq_len // rows),
        in_specs=[spec] * 4 + [_resident((SUB * HEAD_DIM, CHUNK))],
        out_specs=spec,
        out_shape=jax.ShapeDtypeStruct(q.shape, F32),
        scratch_shapes=[pltpu.VMEM((HEADS, HEAD_DIM, HEAD_DIM), F32)],
        compiler_params=_params(2),
        name="hgrn_rec",
    )(q, k, v, g, _sub_block_column_selector())


def _out_mlp_kernel(o_ref, sg_ref, h_ref, mw_ref, wout_ref, nmlp_ref, wup_ref, wdown_ref,
                    nfin_ref, out_ref, *, per_head_norm, final_norm):
    if per_head_norm:
        y = jnp.concatenate([_rms(o_ref[h], mw_ref[...]) for h in range(HEADS)], axis=1)
    else:
        y = _rms(jnp.concatenate([o_ref[h] for h in range(HEADS)], axis=1), mw_ref[...])
    y = y * sg_ref[...]
    h1 = h_ref[...] + jnp.dot(y.astype(BF16), wout_ref[...], preferred_element_type=F32)
    xn = _rms(h1, nmlp_ref[...]).astype(BF16)
    acc = h1
    for c in range(MLP_HIDDEN // D_MODEL):
        cs = slice(c * D_MODEL, (c + 1) * D_MODEL)
        up = jnp.maximum(jnp.dot(xn, wup_ref[:, cs], preferred_element_type=F32), 0.0)
        acc = acc + jnp.dot((up * up).astype(BF16), wdown_ref[cs, :], preferred_element_type=F32)
    if final_norm:
        acc = _rms(acc, nfin_ref[...])
    out_ref[...] = acc


def _out_mlp(o, sg, h, mix_norm_w, w_out, norm_mlp, w_up, w_down, norm_final,
             per_head_norm, final_norm):
    n = h.shape[0]
    rows = ROWS_OUT
    row_spec = pl.BlockSpec((rows, D_MODEL), lambda i: (i, 0))
    mw = mix_norm_w.reshape(1, -1).astype(F32)
    return pl.pallas_call(
        functools.partial(_out_mlp_kernel, per_head_norm=per_head_norm, final_norm=final_norm),
        grid=(n // rows,),
        in_specs=[pl.BlockSpec((HEADS, rows, HEAD_DIM), lambda i: (0, i, 0)), row_spec, row_spec,
                  _resident(mw.shape), _resident((D_MODEL, D_MODEL)), _resident((1, D_MODEL)),
                  _resident((D_MODEL, MLP_HIDDEN)), _resident((MLP_HIDDEN, D_MODEL)),
                  _resident((1, D_MODEL))],
        out_specs=row_spec,
        out_shape=jax.ShapeDtypeStruct((n, D_MODEL), F32),
        compiler_params=_params(1),
        name="out_mlp",
    )(o, sg, h, mw, w_out.astype(BF16), norm_mlp.reshape(1, D_MODEL), w_up.astype(BF16),
      w_down.astype(BF16), norm_final.reshape(1, D_MODEL))


def kernel(x, gdn_w_in, gdn_conv, gdn_a_log, gdn_dt_bias, gdn_onorm, gdn_w_out, hgrn_w_in, hgrn_lb_logits, hgrn_gnorm, hgrn_w_out, norm_mix, norm_mlp, mlp_w_up, mlp_w_down, norm_final):
    batch, seq_len, d_model = x.shape
    depth = norm_mix.shape[0]
    assert d_model == D_MODEL and seq_len % max(ROWS_IN, ROWS_REC, ROWS_OUT) == 0
    h = x.reshape(batch * seq_len, d_model)
    for i in range(depth):
        j = i // 2
        if i % 2 == 0:
            q, k, v, sg, g, beta = _gdn_in(h, norm_mix[i], gdn_w_in[j], gdn_conv[j],
                                           gdn_a_log[j], gdn_dt_bias[j], seq_len)
            o = _gdn_rec(q, k, v, g, beta, batch, seq_len)
            mix_norm_w, w_out = gdn_onorm[j], gdn_w_out[j]
        else:
            q, k, v, g, sg = _hgrn_in(h, norm_mix[i], hgrn_w_in[j], hgrn_lb_logits, i)
            o = _hgrn_rec(q, k, v, g, batch, seq_len)
            mix_norm_w, w_out = hgrn_gnorm[j], hgrn_w_out[j]
        h = _out_mlp(o, sg, h, mix_norm_w, w_out, norm_mlp[i], mlp_w_up[i], mlp_w_down[i],
                     norm_final, per_head_norm=(i % 2 == 0), final_norm=(i == depth - 1))
    return h.reshape(batch, seq_len, d_model)
```

```python
import functools

import numpy as np
import jax
import jax.numpy as jnp
from jax import lax
from jax.experimental import pallas as pl
from jax.experimental.pallas import tpu as pltpu

F32 = jnp.float32
BF16 = jnp.bfloat16

D_MODEL = 1024
HEADS = 8
HEAD_DIM = 128
CHUNK = 64
SUB = 16
N_SUB = CHUNK // SUB
CONV_K = 4
MLP_HIDDEN = 4 * D_MODEL
EPS = 1e-6
LOG2_E = 1.4426950408889634
LANES = 128
HALO_ROWS = 8
COL_GROUP = 256
HEADS_PER_GROUP = COL_GROUP // HEAD_DIM
GDN_PROJ_WIDTH = 4 * D_MODEL + LANES
V7X_VMEM_LIMIT_BYTES = 56 * 1024 * 1024

ROWS_IN = 256
ROWS_REC = 256
ROWS_OUT = 256


def _resident(shape):
    nd = len(shape)
    return pl.BlockSpec(shape, lambda *_: (0,) * nd, pipeline_mode=pl.Buffered(1))


def _params(n_axes):
    return pltpu.CompilerParams(
        dimension_semantics=("arbitrary",) * n_axes,
        vmem_limit_bytes=V7X_VMEM_LIMIT_BYTES)


def _dot(a, b):
    return jnp.dot(a.astype(BF16), b.astype(BF16), preferred_element_type=F32)


def _dot_nt(a, b):
    return lax.dot_general(a.astype(BF16), b.astype(BF16),
                           (((1,), (1,)), ((), ())), preferred_element_type=F32)


def _dot_tn(a, b):
    return lax.dot_general(a.astype(BF16), b.astype(BF16),
                           (((0,), (0,)), ((), ())), preferred_element_type=F32)


def _split2(x):
    hi = x.astype(BF16)
    lo = (x - hi.astype(F32)).astype(BF16)
    return hi, lo


def _dot_split(a_split, b_split):
    a_hi, a_lo = a_split
    b_hi, b_lo = b_split
    n = b_hi.shape[1]
    rhs = jnp.concatenate([jnp.concatenate([b_hi, b_lo], axis=1),
                           jnp.concatenate([b_hi, jnp.zeros_like(b_lo)], axis=1)], axis=0)
    out = jnp.dot(jnp.concatenate([a_hi, a_lo], axis=1), rhs, preferred_element_type=F32)
    return out[:, :n] + out[:, n:]


def _tril3():
    row = lax.broadcasted_iota(jnp.int32, (CHUNK, 3 * CHUNK), 0)
    col = lax.broadcasted_iota(jnp.int32, (CHUNK, 3 * CHUNK), 1)
    return jnp.where(row >= (col & (CHUNK - 1)), 1.0, 0.0).astype(BF16)


def _cumsum_heads(tril3, tiles):
    x = jnp.concatenate(tiles, axis=1)
    x1 = x.astype(BF16)
    r1 = x - x1.astype(F32)
    x2 = r1.astype(BF16)
    x3 = (r1 - x2.astype(F32)).astype(BF16)
    out = jnp.dot(tril3, jnp.concatenate([x1, x2, x3], axis=0), preferred_element_type=F32)
    return [out[:, h * LANES:(h + 1) * LANES] for h in range(len(tiles))]


def _rms(x, w):
    ms = jnp.mean(x * x, axis=-1, keepdims=True)
    return x * lax.rsqrt(ms + EPS) * w


def _sigmoid(x):
    return 1.0 / (1.0 + jnp.exp(-x))


def _silu(x, scale=1.0):
    half = x * 0.5
    hs = half if scale == 1.0 else x * (0.5 * scale)
    return hs + hs * jnp.tanh(half)


def _softplus(x):
    return jnp.maximum(x, 0.0) + jnp.log(1.0 + jnp.exp(-jnp.abs(x)))


def _gdn_in_kernel(h_ref, nw_ref, w_ref, cw_ref, alog_ref, dtb_ref,
                   q_ref, k_ref, v_ref, sg_ref, g_ref, b_ref,
                   pbuf, halo, *, rows, tiles_per_seq):
    @pl.when(pl.program_id(0) % tiles_per_seq == 0)
    def _():
        halo[...] = jnp.zeros_like(halo)

    xn = _rms(h_ref[...], nw_ref[...]).astype(BF16)

    for ci, out_ref in enumerate((q_ref, k_ref, v_ref)):
        for grp in range(D_MODEL // COL_GROUP):
            cs = slice(ci * D_MODEL + grp * COL_GROUP, ci * D_MODEL + (grp + 1) * COL_GROUP)
            p = jnp.dot(xn, w_ref[:, cs], preferred_element_type=F32)
            pbuf[0:HALO_ROWS, cs] = halo[:, cs]
            pbuf[HALO_ROWS:HALO_ROWS + rows, cs] = p
            halo[:, cs] = p[rows - HALO_ROWS:rows, :]
            acc = cw_ref[CONV_K - 1:CONV_K, cs] * p
            for tap in range(CONV_K - 1):
                start = HALO_ROWS - (CONV_K - 1) + tap
                acc = acc + cw_ref[tap:tap + 1, cs] * pbuf[start:start + rows, cs]
            s = _silu(acc)
            for hh in range(HEADS_PER_GROUP):
                sl = s[:, hh * HEAD_DIM:(hh + 1) * HEAD_DIM]
                if ci < 2:
                    ss = jnp.sum(sl * sl, axis=-1, keepdims=True)
                    sl = sl * (lax.rsqrt(ss + EPS) * (HEAD_DIM ** -0.5 if ci == 0 else 1.0))
                out_ref[grp * HEADS_PER_GROUP + hh] = sl

    for grp in range(D_MODEL // COL_GROUP):
        cs = slice(grp * COL_GROUP, (grp + 1) * COL_GROUP)
        pg = jnp.dot(xn, w_ref[:, 3 * D_MODEL + cs.start:3 * D_MODEL + cs.stop],
                     preferred_element_type=F32)
        sg_ref[:, cs] = _silu(pg)

    pab = jnp.dot(xn, w_ref[:, 4 * D_MODEL:GDN_PROJ_WIDTH], preferred_element_type=F32)
    gv = -jnp.exp(alog_ref[...]) * _softplus(pab + dtb_ref[...])
    bv = _sigmoid(pab)
    for h in range(HEADS):
        g_ref[h] = jnp.broadcast_to(gv[:, h:h + 1], (rows, LANES))
        b_ref[h] = jnp.broadcast_to(bv[:, HEADS + h:HEADS + h + 1], (rows, LANES))


def _gdn_in(h, norm_w, w_in, conv_w, a_log, dt_bias, seq_len):
    n = h.shape[0]
    rows = ROWS_IN
    w = jnp.pad(w_in, ((0, 0), (0, GDN_PROJ_WIDTH - w_in.shape[1]))).astype(BF16)
    alog = jnp.pad(a_log.astype(F32), (0, LANES - HEADS)).reshape(1, LANES)
    dtb = jnp.pad(dt_bias.astype(F32), (0, LANES - HEADS)).reshape(1, LANES)
    head_major = jax.ShapeDtypeStruct((HEADS, n, HEAD_DIM), F32)
    hm_spec = pl.BlockSpec((HEADS, rows, HEAD_DIM), lambda i: (0, i, 0))
    row_spec = pl.BlockSpec((rows, D_MODEL), lambda i: (i, 0))
    return pl.pallas_call(
        functools.partial(_gdn_in_kernel, rows=rows, tiles_per_seq=seq_len // rows),
        grid=(n // rows,),
        in_specs=[row_spec, _resident((1, D_MODEL)), _resident((D_MODEL, GDN_PROJ_WIDTH)),
                  _resident((CONV_K, 3 * D_MODEL)), _resident((1, LANES)), _resident((1, LANES))],
        out_specs=[hm_spec, hm_spec, hm_spec, row_spec, hm_spec, hm_spec],
        out_shape=[head_major, head_major, head_major,
                   jax.ShapeDtypeStruct((n, D_MODEL), F32), head_major, head_major],
        scratch_shapes=[pltpu.VMEM((HALO_ROWS + rows, 3 * D_MODEL), F32),
                        pltpu.VMEM((HALO_ROWS, 3 * D_MODEL), F32)],
        compiler_params=_params(1),
        name="gdn_in",
    )(h, norm_w.reshape(1, D_MODEL), w, conv_w, alog, dtb)


def _gdn_rec_kernel(q_ref, k_ref, v_ref, g_ref, b_ref, o_ref, s_ref, *, n_chunks):
    @pl.when(pl.program_id(1) == 0)
    def _():
        s_ref[...] = jnp.zeros_like(s_ref)

    pair = 2 * CHUNK
    n_pairs = HEADS // 2
    tril3 = _tril3()
    row = lax.broadcasted_iota(jnp.int32, (pair, pair), 0)
    col = lax.broadcasted_iota(jnp.int32, (pair, pair), 1)
    first_head = row < CHUNK
    same_head = first_head == (col < CHUNK)
    causal = same_head & (row >= col)
    strict = same_head & (row > col)
    eye = jnp.where(row == col, 1.0, 0.0).astype(F32)

    def halves(x):
        return x[:CHUNK], x[CHUNK:]

    items = [(c, p) for c in range(n_chunks) for p in range(n_pairs)]

    def load_pairs(ref):
        return [jnp.concatenate([ref[2 * p, c * CHUNK:(c + 1) * CHUNK, :],
                                 ref[2 * p + 1, c * CHUNK:(c + 1) * CHUNK, :]], axis=0)
                for c, p in items]

    q, k, v, beta = (load_pairs(r) for r in (q_ref, k_ref, v_ref, b_ref))
    gc_h = _cumsum_heads(tril3,[g_ref[h, c * CHUNK:(c + 1) * CHUNK, :]
                                       for c in range(n_chunks) for h in range(HEADS)])
    gc = [jnp.concatenate([gc_h[c * HEADS + 2 * p], gc_h[c * HEADS + 2 * p + 1]], axis=0)
          for c, p in items]
    decay = [jnp.exp(jnp.where(causal, x - x.T, -jnp.inf)) for x in gc]
    kb = [a * b for a, b in zip(k, beta)]
    kk_qk = [_dot_nt(b, jnp.concatenate([a, c_], axis=0)) for a, b, c_ in zip(kb, k, q)]
    lmat = [jnp.where(strict, x[:, :pair].T * d, 0.0) for x, d in zip(kk_qk, decay)]
    a_qk = [jnp.where(causal, x[:, pair:].T * d, 0.0) for x, d in zip(kk_qk, decay)]
    egc = [jnp.exp(x) for x in gc]
    rhs = [jnp.concatenate([a * b, kb_ * e], axis=1) for a, b, kb_, e in zip(v, beta, kb, egc)]
    pw_s = [_split2(-x) for x in lmat]
    inv = [eye - x for x in lmat]
    for _ in range(5):
        pw = [_dot_split(s, s) for s in pw_s]
        pw_s = [_split2(x) for x in pw]
        inv = [i + _dot_split(_split2(i), s) for i, s in zip(inv, pw_s)]
    sol = [_dot_split(_split2(i), _split2(r)) for i, r in zip(inv, rhs)]
    qe = [a * e for a, e in zip(q, egc)]
    g_last = [jnp.where(first_head, x[CHUNK - 1:CHUNK, :], x[pair - 1:pair, :]) for x in gc]
    k_dec = [a * jnp.exp(gl - x) for a, gl, x in zip(k, g_last, gc)]

    state = [s_ref[h] for h in range(HEADS)]
    for c in range(n_chunks):
        ps = range(c * n_pairs, (c + 1) * n_pairs)
        w_h = [t for i in ps for t in halves(sol[i][:, HEAD_DIM:])]
        qe_h = [t for i in ps for t in halves(qe[i])]
        wq_s = [_dot(jnp.concatenate([a, b], axis=0), s) for a, b, s in zip(w_h, qe_h, state)]
        w_s = [x[:CHUNK] for x in wq_s]
        q_s = [x[CHUNK:] for x in wq_s]
        v_new = [sol[i][:, :HEAD_DIM] - jnp.concatenate([w_s[2 * p], w_s[2 * p + 1]], axis=0)
                 for p, i in enumerate(ps)]
        o = [jnp.concatenate([q_s[2 * p], q_s[2 * p + 1]], axis=0) + _dot(a_qk[i], v_new[p])
             for p, i in enumerate(ps)]
        for p in range(n_pairs):
            o_ref[2 * p, c * CHUNK:(c + 1) * CHUNK, :] = o[p][:CHUNK]
            o_ref[2 * p + 1, c * CHUNK:(c + 1) * CHUNK, :] = o[p][CHUNK:]
        upd = [_dot_tn(a, b) for a, b in zip([t for i in ps for t in halves(k_dec[i])],
                                             [t for x in v_new for t in halves(x)])]
        state = [s * jnp.exp(gc_h[c * HEADS + h][CHUNK - 1:CHUNK, :]) + d
                 for h, (s, d) in enumerate(zip(state, upd))]
    for h in range(HEADS):
        s_ref[h] = state[h]


def _gdn_rec(q, k, v, g, beta, batch, seq_len):
    rows = ROWS_REC
    spec = pl.BlockSpec((HEADS, rows, HEAD_DIM),
                        lambda b, t: (0, b * (seq_len // rows) + t, 0))
    return pl.pallas_call(
        functools.partial(_gdn_rec_kernel, n_chunks=rows // CHUNK),
        grid=(batch, seq_len // rows),
        in_specs=[spec] * 5,
        out_specs=spec,
        out_shape=jax.ShapeDtypeStruct(q.shape, F32),
        scratch_shapes=[pltpu.VMEM((HEADS, HEAD_DIM, HEAD_DIM), F32)],
        compiler_params=_params(2),
        name="gdn_rec",
    )(q, k, v, g, beta)


def _hgrn_in_kernel(h_ref, nw_ref, w_ref, lg_ref,
                    q_ref, k_ref, v_ref, g_ref, sg_ref, *, layer):
    lg = lg_ref[...]
    m = jnp.max(lg, axis=0, keepdims=True)
    e = jnp.exp(lg - m)
    sm = e / jnp.sum(e, axis=0, keepdims=True)
    cs = sm[0:1, :]
    for r in range(1, layer + 1):
        cs = cs + sm[r:r + 1, :]
    lb = cs - sm[0:1, :]
    log_lb = jnp.log(lb)
    log_1m_lb = jnp.log1p(-lb)

    xn = _rms(h_ref[...], nw_ref[...]).astype(BF16)

    def proj(i, cs):
        return jnp.dot(xn, w_ref[:, i * D_MODEL + cs.start:i * D_MODEL + cs.stop],
                       preferred_element_type=F32)

    def heads_out(ref, grp, val):
        for hh in range(HEADS_PER_GROUP):
            ref[grp * HEADS_PER_GROUP + hh] = val[:, hh * HEAD_DIM:(hh + 1) * HEAD_DIM]

    def finish(kind, grp, p):
        cs = slice(grp * COL_GROUP, (grp + 1) * COL_GROUP)
        if kind == 0:
            heads_out(q_ref, grp, _silu(p, HEAD_DIM ** -0.5))
        elif kind == 1:
            t = jnp.exp(-jnp.abs(p))
            log_sig = jnp.minimum(p, 0.0) - jnp.log(1.0 + t)
            a = log_lb[:, cs]
            b = log_1m_lb[:, cs] + log_sig
            heads_out(g_ref, grp, jnp.maximum(a, b) + jnp.log(1.0 + jnp.exp(-jnp.abs(a - b))))
            inv = 1.0 / (1.0 + t)
            heads_out(k_ref, grp, (1.0 - lb[:, cs]) * jnp.where(p >= 0.0, t * inv, inv))
        elif kind == 2:
            heads_out(v_ref, grp, p)
        else:
            sg_ref[:, cs] = _silu(p)

    pending = None
    for grp in range(D_MODEL // COL_GROUP):
        for kind in (1, 0, 3, 2):
            p = proj(kind, slice(grp * COL_GROUP, (grp + 1) * COL_GROUP))
            if pending is not None:
                finish(*pending)
            pending = (kind, grp, p)
    finish(*pending)


def _hgrn_in(h, norm_w, w_in, lb_logits, layer):
    n = h.shape[0]
    rows = ROWS_IN
    depth = lb_logits.shape[0]
    head_major = jax.ShapeDtypeStruct((HEADS, n, HEAD_DIM), F32)
    hm_spec = pl.BlockSpec((HEADS, rows, HEAD_DIM), lambda i: (0, i, 0))
    row_spec = pl.BlockSpec((rows, D_MODEL), lambda i: (i, 0))
    return pl.pallas_call(
        functools.partial(_hgrn_in_kernel, layer=layer),
        grid=(n // rows,),
        in_specs=[row_spec, _resident((1, D_MODEL)), _resident((D_MODEL, 4 * D_MODEL)),
                  _resident((depth, D_MODEL))],
        out_specs=[hm_spec, hm_spec, hm_spec, hm_spec, row_spec],
        out_shape=[head_major] * 4 + [jax.ShapeDtypeStruct((n, D_MODEL), F32)],
        compiler_params=_params(1),
        name="hgrn_in",
    )(h, norm_w.reshape(1, D_MODEL), w_in.astype(BF16), lb_logits.astype(F32))


def _hgrn_rec_kernel(q_ref, k_ref, v_ref, g_ref, ecol_ref, o_ref, s_ref, *, n_chunks):
    @pl.when(pl.program_id(1) == 0)
    def _():
        s_ref[...] = jnp.zeros_like(s_ref)

    tril3 = _tril3()
    row = lax.broadcasted_iota(jnp.int32, (CHUNK, CHUNK), 0)
    col = lax.broadcasted_iota(jnp.int32, (CHUNK, CHUNK), 1)
    same_sub = ((row // SUB) == (col // SUB)) & (row >= col)
    sub_i = lax.broadcasted_iota(jnp.int32, (N_SUB, SUB, HEAD_DIM), 1)
    heads = range(HEADS)

    def diag_products(q, k, gc):
        g3 = (gc * LOG2_E).reshape(N_SUB, SUB, HEAD_DIM)
        q3 = q.reshape(N_SUB, SUB, HEAD_DIM)
        k3 = k.reshape(N_SUB, SUB, HEAD_DIM)
        half = SUB // 2
        zeros = jnp.zeros((N_SUB, half, HEAD_DIM), F32)
        pieces = []
        for j in range(SUB):
            lo = 0 if j < half else half
            dec = jnp.exp2(jnp.minimum(g3[:, lo:, :] - g3[:, j:j + 1, :], 0.0))
            piece = q3[:, lo:, :] * dec * k3[:, j:j + 1, :]
            if lo:
                piece = jnp.concatenate([zeros, piece], axis=1)
            pieces.append(piece.reshape(CHUNK, HEAD_DIM).astype(BF16))
        return jnp.concatenate(pieces, axis=1)

    items = [(c, h) for c in range(n_chunks) for h in heads]

    def tiles(ref):
        return [ref[h, c * CHUNK:(c + 1) * CHUNK, :] for c, h in items]

    q, k, v, g = (tiles(r) for r in (q_ref, k_ref, v_ref, g_ref))
    gc = _cumsum_heads(tril3,g)

    a_mat = []
    for c in range(n_chunks):
        ids = range(c * HEADS, (c + 1) * HEADS)
        prods = jnp.concatenate([diag_products(q[i], k[i], gc[i]) for i in ids], axis=0)
        a_all = jnp.dot(prods, ecol_ref[...], preferred_element_type=F32)
        a_mat += [jnp.where(same_sub, a_all[h * CHUNK:(h + 1) * CHUNK], 0.0) for h in heads]

    gb = [x - y for x, y in zip(gc, g)]
    a_rows = [[jnp.zeros((SUB, CHUNK), F32)] for _ in items]
    for s in range(1, N_SUB):
        lo, hi = s * SUB, (s + 1) * SUB
        for i in range(len(items)):
            gb_s = gb[i][lo:lo + 1, :]
            q_off = q[i][lo:hi, :] * jnp.exp(gc[i][lo:hi, :] - gb_s)
            k_off = jnp.concatenate(
                [k[i][:lo, :] * jnp.exp(gb_s - gc[i][:lo, :]),
                 jnp.zeros((CHUNK - lo, HEAD_DIM), F32)], axis=0)
            a_rows[i].append(_dot_nt(q_off, k_off))
    a_mat = [a + jnp.concatenate(r, axis=0) for a, r in zip(a_mat, a_rows)]
    o_intra = [_dot(a, b) for a, b in zip(a_mat, v)]
    qe = [a * jnp.exp(x) for a, x in zip(q, gc)]
    g_last = [x[CHUNK - 1:CHUNK, :] for x in gc]
    upd = [_dot_tn(a, b * jnp.exp(gl - x)) for a, b, gl, x in zip(v, k, g_last, gc)]

    state_t = [s_ref[h] for h in heads]
    for c in range(n_chunks):
        ids = range(c * HEADS, (c + 1) * HEADS)
        o_state = [_dot_nt(qe[i], s) for i, s in zip(ids, state_t)]
        for h, i in enumerate(ids):
            o_ref[h, c * CHUNK:(c + 1) * CHUNK, :] = o_intra[i] + o_state[h]
        state_t = [s * jnp.exp(g_last[i]) + upd[i] for i, s in zip(ids, state_t)]
    for h in heads:
        s_ref[h] = state_t[h]


def _sub_block_column_selector():
    j = np.arange(SUB * HEAD_DIM) // HEAD_DIM
    c = np.arange(CHUNK) % SUB
    return jnp.asarray(j[:, None] == c[None, :], dtype=BF16)


def _hgrn_rec(q, k, v, g, batch, seq_len):
    rows = ROWS_REC
    spec = pl.BlockSpec((HEADS, rows, HEAD_DIM),
                        lambda b, t: (0, b * (seq_len // rows) + t, 0))
    return pl.pallas_call(
        functools.partial(_hgrn_rec_kernel, n_chunks=rows // CHUNK),
        grid=(batch, seq_len // rows),
        in_specs=[spec] * 4 + [_resident((SUB * HEAD_DIM, CHUNK))],
        out_specs=spec,
        out_shape=jax.ShapeDtypeStruct(q.shape, F32),
        scratch_shapes=[pltpu.VMEM((HEADS, HEAD_DIM, HEAD_DIM), F32)],
        compiler_params=_params(2),
        name="hgrn_rec",
    )(q, k, v, g, _sub_block_column_selector())


def _out_mlp_kernel(o_ref, sg_ref, h_ref, mw_ref, wout_ref, nmlp_ref, wup_ref, wdown_ref,
                    nfin_ref, out_ref, *, per_head_norm, final_norm):
    if per_head_norm:
        y = jnp.concatenate([_rms(o_ref[h], mw_ref[...]) for h in range(HEADS)], axis=1)
    else:
        y = _rms(jnp.concatenate([o_ref[h] for h in range(HEADS)], axis=1), mw_ref[...])
    y = y * sg_ref[...]
    h1 = h_ref[...] + jnp.dot(y.astype(BF16), wout_ref[...], preferred_element_type=F32)
    xn = _rms(h1, nmlp_ref[...]).astype(BF16)
    acc = h1
    for c in range(MLP_HIDDEN // D_MODEL):
        cs = slice(c * D_MODEL, (c + 1) * D_MODEL)
        up = jnp.maximum(jnp.dot(xn, wup_ref[:, cs], preferred_element_type=F32), 0.0)
        acc = acc + jnp.dot((up * up).astype(BF16), wdown_ref[cs, :], preferred_element_type=F32)
    if final_norm:
        acc = _rms(acc, nfin_ref[...])
    out_ref[...] = acc


def _out_mlp(o, sg, h, mix_norm_w, w_out, norm_mlp, w_up, w_down, norm_final,
             per_head_norm, final_norm):
    n = h.shape[0]
    rows = ROWS_OUT
    row_spec = pl.BlockSpec((rows, D_MODEL), lambda i: (i, 0))
    mw = mix_norm_w.reshape(1, -1).astype(F32)
    return pl.pallas_call(
        functools.partial(_out_mlp_kernel, per_head_norm=per_head_norm, final_norm=final_norm),
        grid=(n // rows,),
        in_specs=[pl.BlockSpec((HEADS, rows, HEAD_DIM), lambda i: (0, i, 0)), row_spec, row_spec,
                  _resident(mw.shape), _resident((D_MODEL, D_MODEL)), _resident((1, D_MODEL)),
                  _resident((D_MODEL, MLP_HIDDEN)), _resident((MLP_HIDDEN, D_MODEL)),
                  _resident((1, D_MODEL))],
        out_specs=row_spec,
        out_shape=jax.ShapeDtypeStruct((n, D_MODEL), F32),
        compiler_params=_params(1),
        name="out_mlp",
    )(o, sg, h, mw, w_out.astype(BF16), norm_mlp.reshape(1, D_MODEL), w_up.astype(BF16),
      w_down.astype(BF16), norm_final.reshape(1, D_MODEL))


def kernel(x, gdn_w_in, gdn_conv, gdn_a_log, gdn_dt_bias, gdn_onorm, gdn_w_out, hgrn_w_in, hgrn_lb_logits, hgrn_gnorm, hgrn_w_out, norm_mix, norm_mlp, mlp_w_up, mlp_w_down, norm_final):
    batch, seq_len, d_model = x.shape
    depth = norm_mix.shape[0]
    assert d_model == D_MODEL and seq_len % max(ROWS_IN, ROWS_REC, ROWS_OUT) == 0
    h = x.reshape(batch * seq_len, d_model)
    for i in range(depth):
        j = i // 2
        if i % 2 == 0:
            q, k, v, sg, g, beta = _gdn_in(h, norm_mix[i], gdn_w_in[j], gdn_conv[j],
                                           gdn_a_log[j], gdn_dt_bias[j], seq_len)
            o = _gdn_rec(q, k, v, g, beta, batch, seq_len)
            mix_norm_w, w_out = gdn_onorm[j], gdn_w_out[j]
        else:
            q, k, v, g, sg = _hgrn_in(h, norm_mix[i], hgrn_w_in[j], hgrn_lb_logits, i)
            o = _hgrn_rec(q, k, v, g, batch, seq_len)
            mix_norm_w, w_out = hgrn_gnorm[j], hgrn_w_out[j]
        h = _out_mlp(o, sg, h, mix_norm_w, w_out, norm_mlp[i], mlp_w_up[i], mlp_w_down[i],
                     norm_final, per_head_norm=(i % 2 == 0), final_norm=(i == depth - 1))
    return h.reshape(batch, seq_len, d_model)
```

```python
import functools

import numpy as np
import jax
import jax.numpy as jnp
from jax import lax
from jax.experimental import pallas as pl
from jax.experimental.pallas import tpu as pltpu

F32 = jnp.float32
BF16 = jnp.bfloat16

D_MODEL = 1024
HEADS = 8
HEAD_DIM = 128
CHUNK = 64
SUB = 16
N_SUB = CHUNK // SUB
CONV_K = 4
MLP_HIDDEN = 4 * D_MODEL
EPS = 1e-6
LOG2_E = 1.4426950408889634
LANES = 128
HALO_ROWS = 8
COL_GROUP = 256
HEADS_PER_GROUP = COL_GROUP // HEAD_DIM
GDN_PROJ_WIDTH = 4 * D_MODEL + LANES
V7X_VMEM_LIMIT_BYTES = 56 * 1024 * 1024

ROWS_IN = 256
ROWS_REC = 256


def _resident(shape):
    nd = len(shape)
    return pl.BlockSpec(shape, lambda *_: (0,) * nd, pipeline_mode=pl.Buffered(1))


def _params(n_axes):
    return pltpu.CompilerParams(
        dimension_semantics=("arbitrary",) * n_axes,
        vmem_limit_bytes=V7X_VMEM_LIMIT_BYTES)


def _dot(a, b):
    return jnp.dot(a.astype(BF16), b.astype(BF16), preferred_element_type=F32)


def _dot_nt(a, b):
    return lax.dot_general(a.astype(BF16), b.astype(BF16),
                           (((1,), (1,)), ((), ())), preferred_element_type=F32)


def _dot_tn(a, b):
    return lax.dot_general(a.astype(BF16), b.astype(BF16),
                           (((0,), (0,)), ((), ())), preferred_element_type=F32)


def _split2(x):
    hi = x.astype(BF16)
    lo = (x - hi.astype(F32)).astype(BF16)
    return hi, lo


def _dot_split(a_split, b_split):
    a_hi, a_lo = a_split
    b_hi, b_lo = b_split
    n = b_hi.shape[1]
    rhs = jnp.concatenate([jnp.concatenate([b_hi, b_lo], axis=1),
                           jnp.concatenate([b_hi, jnp.zeros_like(b_lo)], axis=1)], axis=0)
    out = jnp.dot(jnp.concatenate([a_hi, a_lo], axis=1), rhs, preferred_element_type=F32)
    return out[:, :n] + out[:, n:]


def _tril3():
    row = lax.broadcasted_iota(jnp.int32, (CHUNK, 3 * CHUNK), 0)
    col = lax.broadcasted_iota(jnp.int32, (CHUNK, 3 * CHUNK), 1)
    return jnp.where(row >= (col & (CHUNK - 1)), 1.0, 0.0).astype(BF16)


def _cumsum_heads(tril3, tiles):
    x = jnp.concatenate(tiles, axis=1)
    x1 = x.astype(BF16)
    r1 = x - x1.astype(F32)
    x2 = r1.astype(BF16)
    x3 = (r1 - x2.astype(F32)).astype(BF16)
    out = jnp.dot(tril3, jnp.concatenate([x1, x2, x3], axis=0), preferred_element_type=F32)
    return [out[:, h * LANES:(h + 1) * LANES] for h in range(len(tiles))]


def _rms(x, w):
    ms = jnp.mean(x * x, axis=-1, keepdims=True)
    return x * lax.rsqrt(ms + EPS) * w


def _sigmoid(x):
    return 1.0 / (1.0 + jnp.exp(-x))


def _silu(x, scale=1.0):
    half = x * 0.5
    hs = half if scale == 1.0 else x * (0.5 * scale)
    return hs + hs * jnp.tanh(half)


def _softplus(x):
    return jnp.maximum(x, 0.0) + jnp.log(1.0 + jnp.exp(-jnp.abs(x)))


def _gdn_in_kernel(h_ref, nw_ref, w_ref, cw_ref, alog_ref, dtb_ref,
                   q_ref, k_ref, v_ref, sg_ref, g_ref, b_ref,
                   pbuf, halo, *, rows, tiles_per_seq):
    @pl.when(pl.program_id(0) % tiles_per_seq == 0)
    def _():
        halo[...] = jnp.zeros_like(halo)

    xn = _rms(h_ref[...], nw_ref[...]).astype(BF16)

    for ci, out_ref in enumerate((q_ref, k_ref, v_ref)):
        for grp in range(D_MODEL // COL_GROUP):
            cs = slice(ci * D_MODEL + grp * COL_GROUP, ci * D_MODEL + (grp + 1) * COL_GROUP)
            p = jnp.dot(xn, w_ref[:, cs], preferred_element_type=F32)
            pbuf[0:HALO_ROWS, cs] = halo[:, cs]
            pbuf[HALO_ROWS:HALO_ROWS + rows, cs] = p
            halo[:, cs] = p[rows - HALO_ROWS:rows, :]
            acc = cw_ref[CONV_K - 1:CONV_K, cs] * p
            for tap in range(CONV_K - 1):
                start = HALO_ROWS - (CONV_K - 1) + tap
                acc = acc + cw_ref[tap:tap + 1, cs] * pbuf[start:start + rows, cs]
            s = _silu(acc)
            for hh in range(HEADS_PER_GROUP):
                sl = s[:, hh * HEAD_DIM:(hh + 1) * HEAD_DIM]
                if ci < 2:
                    ss = jnp.sum(sl * sl, axis=-1, keepdims=True)
                    sl = sl * (lax.rsqrt(ss + EPS) * (HEAD_DIM ** -0.5 if ci == 0 else 1.0))
                out_ref[grp * HEADS_PER_GROUP + hh] = sl

    for grp in range(D_MODEL // COL_GROUP):
        cs = slice(grp * COL_GROUP, (grp + 1) * COL_GROUP)
        pg = jnp.dot(xn, w_ref[:, 3 * D_MODEL + cs.start:3 * D_MODEL + cs.stop],
                     preferred_element_type=F32)
        sg_ref[:, cs] = _silu(pg)

    pab = jnp.dot(xn, w_ref[:, 4 * D_MODEL:GDN_PROJ_WIDTH], preferred_element_type=F32)
    gv = -jnp.exp(alog_ref[...]) * _softplus(pab + dtb_ref[...])
    bv = _sigmoid(pab)
    for h in range(HEADS):
        g_ref[h] = jnp.broadcast_to(gv[:, h:h + 1], (rows, LANES))
        b_ref[h] = jnp.broadcast_to(bv[:, HEADS + h:HEADS + h + 1], (rows, LANES))


def _gdn_in(h, norm_w, w_in, conv_w, a_log, dt_bias, seq_len):
    n = h.shape[0]
    rows = ROWS_IN
    w = jnp.pad(w_in, ((0, 0), (0, GDN_PROJ_WIDTH - w_in.shape[1]))).astype(BF16)
    alog = jnp.pad(a_log.astype(F32), (0, LANES - HEADS)).reshape(1, LANES)
    dtb = jnp.pad(dt_bias.astype(F32), (0, LANES - HEADS)).reshape(1, LANES)
    head_major = jax.ShapeDtypeStruct((HEADS, n, HEAD_DIM), F32)
    hm_spec = pl.BlockSpec((HEADS, rows, HEAD_DIM), lambda i: (0, i, 0))
    row_spec = pl.BlockSpec((rows, D_MODEL), lambda i: (i, 0))
    return pl.pallas_call(
        functools.partial(_gdn_in_kernel, rows=rows, tiles_per_seq=seq_len // rows),
        grid=(n // rows,),
        in_specs=[row_spec, _resident((1, D_MODEL)), _resident((D_MODEL, GDN_PROJ_WIDTH)),
                  _resident((CONV_K, 3 * D_MODEL)), _resident((1, LANES)), _resident((1, LANES))],
        out_specs=[hm_spec, hm_spec, hm_spec, row_spec, hm_spec, hm_spec],
        out_shape=[head_major, head_major, head_major,
                   jax.ShapeDtypeStruct((n, D_MODEL), F32), head_major, head_major],
        scratch_shapes=[pltpu.VMEM((HALO_ROWS + rows, 3 * D_MODEL), F32),
                        pltpu.VMEM((HALO_ROWS, 3 * D_MODEL), F32)],
        compiler_params=_params(1),
        name="gdn_in",
    )(h, norm_w.reshape(1, D_MODEL), w, conv_w, alog, dtb)


def _gdn_rec_tile(q_ref, k_ref, v_ref, g_ref, b_ref, o_ref, s_ref, *, n_chunks, first_tile):
    @pl.when(first_tile)
    def _():
        s_ref[...] = jnp.zeros_like(s_ref)

    pair = 2 * CHUNK
    n_pairs = HEADS // 2
    tril3 = _tril3()
    row = lax.broadcasted_iota(jnp.int32, (pair, pair), 0)
    col = lax.broadcasted_iota(jnp.int32, (pair, pair), 1)
    first_head = row < CHUNK
    same_head = first_head == (col < CHUNK)
    causal = same_head & (row >= col)
    strict = same_head & (row > col)
    eye = jnp.where(row == col, 1.0, 0.0).astype(F32)

    def halves(x):
        return x[:CHUNK], x[CHUNK:]

    items = [(c, p) for c in range(n_chunks) for p in range(n_pairs)]

    def load_pairs(ref):
        return [jnp.concatenate([ref[2 * p, c * CHUNK:(c + 1) * CHUNK, :],
                                 ref[2 * p + 1, c * CHUNK:(c + 1) * CHUNK, :]], axis=0)
                for c, p in items]

    q, k, v, beta = (load_pairs(r) for r in (q_ref, k_ref, v_ref, b_ref))
    gc_h = _cumsum_heads(tril3,[g_ref[h, c * CHUNK:(c + 1) * CHUNK, :]
                                       for c in range(n_chunks) for h in range(HEADS)])
    gc = [jnp.concatenate([gc_h[c * HEADS + 2 * p], gc_h[c * HEADS + 2 * p + 1]], axis=0)
          for c, p in items]
    decay = [jnp.exp(jnp.where(causal, x - x.T, -jnp.inf)) for x in gc]
    kb = [a * b for a, b in zip(k, beta)]
    kk_qk = [_dot_nt(b, jnp.concatenate([a, c_], axis=0)) for a, b, c_ in zip(kb, k, q)]
    lmat = [jnp.where(strict, x[:, :pair].T * d, 0.0) for x, d in zip(kk_qk, decay)]
    a_qk = [jnp.where(causal, x[:, pair:].T * d, 0.0) for x, d in zip(kk_qk, decay)]
    egc = [jnp.exp(x) for x in gc]
    rhs = [jnp.concatenate([a * b, kb_ * e], axis=1) for a, b, kb_, e in zip(v, beta, kb, egc)]
    pw_s = [_split2(-x) for x in lmat]
    inv = [eye - x for x in lmat]
    yield
    for _ in range(5):
        pw = [_dot_split(s, s) for s in pw_s]
        pw_s = [_split2(x) for x in pw]
        inv = [i + _dot_split(_split2(i), s) for i, s in zip(inv, pw_s)]
        yield
    sol = [_dot_split(_split2(i), _split2(r)) for i, r in zip(inv, rhs)]
    qe = [a * e for a, e in zip(q, egc)]
    g_last = [jnp.where(first_head, x[CHUNK - 1:CHUNK, :], x[pair - 1:pair, :]) for x in gc]
    k_dec = [a * jnp.exp(gl - x) for a, gl, x in zip(k, g_last, gc)]
    yield

    state = [s_ref[h] for h in range(HEADS)]
    for c in range(n_chunks):
        ps = range(c * n_pairs, (c + 1) * n_pairs)
        w_h = [t for i in ps for t in halves(sol[i][:, HEAD_DIM:])]
        qe_h = [t for i in ps for t in halves(qe[i])]
        wq_s = [_dot(jnp.concatenate([a, b], axis=0), s) for a, b, s in zip(w_h, qe_h, state)]
        w_s = [x[:CHUNK] for x in wq_s]
        q_s = [x[CHUNK:] for x in wq_s]
        v_new = [sol[i][:, :HEAD_DIM] - jnp.concatenate([w_s[2 * p], w_s[2 * p + 1]], axis=0)
                 for p, i in enumerate(ps)]
        o = [jnp.concatenate([q_s[2 * p], q_s[2 * p + 1]], axis=0) + _dot(a_qk[i], v_new[p])
             for p, i in enumerate(ps)]
        for p in range(n_pairs):
            o_ref[2 * p, c * CHUNK:(c + 1) * CHUNK, :] = o[p][:CHUNK]
            o_ref[2 * p + 1, c * CHUNK:(c + 1) * CHUNK, :] = o[p][CHUNK:]
        upd = [_dot_tn(a, b) for a, b in zip([t for i in ps for t in halves(k_dec[i])],
                                             [t for x in v_new for t in halves(x)])]
        state = [s * jnp.exp(gc_h[c * HEADS + h][CHUNK - 1:CHUNK, :]) + d
                 for h, (s, d) in enumerate(zip(state, upd))]
        if c + 1 < n_chunks:
            yield
    for h in range(HEADS):
        s_ref[h] = state[h]


def _hgrn_in_kernel(h_ref, nw_ref, w_ref, lg_ref,
                    q_ref, k_ref, v_ref, g_ref, sg_ref, *, layer):
    lg = lg_ref[...]
    m = jnp.max(lg, axis=0, keepdims=True)
    e = jnp.exp(lg - m)
    sm = e / jnp.sum(e, axis=0, keepdims=True)
    cs = sm[0:1, :]
    for r in range(1, layer + 1):
        cs = cs + sm[r:r + 1, :]
    lb = cs - sm[0:1, :]
    log_lb = jnp.log(lb)
    log_1m_lb = jnp.log1p(-lb)

    xn = _rms(h_ref[...], nw_ref[...]).astype(BF16)

    def proj(i, cs):
        return jnp.dot(xn, w_ref[:, i * D_MODEL + cs.start:i * D_MODEL + cs.stop],
                       preferred_element_type=F32)

    def heads_out(ref, grp, val):
        for hh in range(HEADS_PER_GROUP):
            ref[grp * HEADS_PER_GROUP + hh] = val[:, hh * HEAD_DIM:(hh + 1) * HEAD_DIM]

    def finish(kind, grp, p):
        cs = slice(grp * COL_GROUP, (grp + 1) * COL_GROUP)
        if kind == 0:
            heads_out(q_ref, grp, _silu(p, HEAD_DIM ** -0.5))
        elif kind == 1:
            t = jnp.exp(-jnp.abs(p))
            log_sig = jnp.minimum(p, 0.0) - jnp.log(1.0 + t)
            a = log_lb[:, cs]
            b = log_1m_lb[:, cs] + log_sig
            heads_out(g_ref, grp, jnp.maximum(a, b) + jnp.log(1.0 + jnp.exp(-jnp.abs(a - b))))
            inv = 1.0 / (1.0 + t)
            heads_out(k_ref, grp, (1.0 - lb[:, cs]) * jnp.where(p >= 0.0, t * inv, inv))
        elif kind == 2:
            heads_out(v_ref, grp, p)
        else:
            sg_ref[:, cs] = _silu(p)

    pending = None
    for grp in range(D_MODEL // COL_GROUP):
        for kind in (1, 0, 3, 2):
            p = proj(kind, slice(grp * COL_GROUP, (grp + 1) * COL_GROUP))
            if pending is not None:
                finish(*pending)
            pending = (kind, grp, p)
    finish(*pending)


def _hgrn_in(h, norm_w, w_in, lb_logits, layer):
    n = h.shape[0]
    rows = ROWS_IN
    depth = lb_logits.shape[0]
    head_major = jax.ShapeDtypeStruct((HEADS, n, HEAD_DIM), F32)
    hm_spec = pl.BlockSpec((HEADS, rows, HEAD_DIM), lambda i: (0, i, 0))
    row_spec = pl.BlockSpec((rows, D_MODEL), lambda i: (i, 0))
    return pl.pallas_call(
        functools.partial(_hgrn_in_kernel, layer=layer),
        grid=(n // rows,),
        in_specs=[row_spec, _resident((1, D_MODEL)), _resident((D_MODEL, 4 * D_MODEL)),
                  _resident((depth, D_MODEL))],
        out_specs=[hm_spec, hm_spec, hm_spec, hm_spec, row_spec],
        out_shape=[head_major] * 4 + [jax.ShapeDtypeStruct((n, D_MODEL), F32)],
        compiler_params=_params(1),
        name="hgrn_in",
    )(h, norm_w.reshape(1, D_MODEL), w_in.astype(BF16), lb_logits.astype(F32))


def _hgrn_rec_tile(q_ref, k_ref, v_ref, g_ref, ecol_ref, o_ref, s_ref, *, n_chunks, first_tile):
    @pl.when(first_tile)
    def _():
        s_ref[...] = jnp.zeros_like(s_ref)

    tril3 = _tril3()
    row = lax.broadcasted_iota(jnp.int32, (CHUNK, CHUNK), 0)
    col = lax.broadcasted_iota(jnp.int32, (CHUNK, CHUNK), 1)
    same_sub = ((row // SUB) == (col // SUB)) & (row >= col)
    sub_i = lax.broadcasted_iota(jnp.int32, (N_SUB, SUB, HEAD_DIM), 1)
    heads = range(HEADS)

    def diag_products(q, k, gc):
        g3 = (gc * LOG2_E).reshape(N_SUB, SUB, HEAD_DIM)
        q3 = q.reshape(N_SUB, SUB, HEAD_DIM)
        k3 = k.reshape(N_SUB, SUB, HEAD_DIM)
        half = SUB // 2
        zeros = jnp.zeros((N_SUB, half, HEAD_DIM), F32)
        pieces = []
        for j in range(SUB):
            lo = 0 if j < half else half
            dec = jnp.exp2(jnp.minimum(g3[:, lo:, :] - g3[:, j:j + 1, :], 0.0))
            piece = q3[:, lo:, :] * dec * k3[:, j:j + 1, :]
            if lo:
                piece = jnp.concatenate([zeros, piece], axis=1)
            pieces.append(piece.reshape(CHUNK, HEAD_DIM).astype(BF16))
        return jnp.concatenate(pieces, axis=1)

    items = [(c, h) for c in range(n_chunks) for h in heads]

    def tiles(ref):
        return [ref[h, c * CHUNK:(c + 1) * CHUNK, :] for c, h in items]

    q, k, v, g = (tiles(r) for r in (q_ref, k_ref, v_ref, g_ref))
    gc = _cumsum_heads(tril3, g)

    gb = [x - y for x, y in zip(gc, g)]
    a_rows = [[jnp.zeros((SUB, CHUNK), F32)] for _ in items]
    for s in range(1, N_SUB):
        lo, hi = s * SUB, (s + 1) * SUB
        for i in range(len(items)):
            gb_s = gb[i][lo:lo + 1, :]
            q_off = q[i][lo:hi, :] * jnp.exp(gc[i][lo:hi, :] - gb_s)
            k_off = jnp.concatenate(
                [k[i][:lo, :] * jnp.exp(gb_s - gc[i][:lo, :]),
                 jnp.zeros((CHUNK - lo, HEAD_DIM), F32)], axis=0)
            a_rows[i].append(_dot_nt(q_off, k_off))
    qe = [a * jnp.exp(x) for a, x in zip(q, gc)]
    g_last = [x[CHUNK - 1:CHUNK, :] for x in gc]
    upd = [_dot_tn(a, b * jnp.exp(gl - x)) for a, b, gl, x in zip(v, k, g_last, gc)]
    yield

    o_intra = []
    for c in range(n_chunks):
        ids = range(c * HEADS, (c + 1) * HEADS)
        prods = jnp.concatenate([diag_products(q[i], k[i], gc[i]) for i in ids], axis=0)
        a_all = jnp.dot(prods, ecol_ref[...], preferred_element_type=F32)
        a_mat = [jnp.where(same_sub, a_all[h * CHUNK:(h + 1) * CHUNK], 0.0)
                 + jnp.concatenate(a_rows[i], axis=0) for h, i in enumerate(ids)]
        o_intra += [_dot(a, v[i]) for a, i in zip(a_mat, ids)]
        yield

    state_t = [s_ref[h] for h in heads]
    for c in range(n_chunks):
        ids = range(c * HEADS, (c + 1) * HEADS)
        o_state = [_dot_nt(qe[i], s) for i, s in zip(ids, state_t)]
        for h, i in enumerate(ids):
            o_ref[h, c * CHUNK:(c + 1) * CHUNK, :] = o_intra[i] + o_state[h]
        state_t = [s * jnp.exp(g_last[i]) + upd[i] for i, s in zip(ids, state_t)]
    for h in heads:
        s_ref[h] = state_t[h]


def _sub_block_column_selector():
    j = np.arange(SUB * HEAD_DIM) // HEAD_DIM
    c = np.arange(CHUNK) % SUB
    return jnp.asarray(j[:, None] == c[None, :], dtype=BF16)


def _out_mlp_tile(o_ref, sg_ref, h_ref, mw_ref, wout_ref, nmlp_ref, wup_ref, wdown_ref,
                  nfin_ref, out_ref, *, per_head_norm, final_norm):
    if per_head_norm:
        y = jnp.concatenate([_rms(o_ref[h], mw_ref[...]) for h in range(HEADS)], axis=1)
    else:
        y = _rms(jnp.concatenate([o_ref[h] for h in range(HEADS)], axis=1), mw_ref[...])
    y = y * sg_ref[...]
    h1 = h_ref[...] + jnp.dot(y.astype(BF16), wout_ref[...], preferred_element_type=F32)
    xn = _rms(h1, nmlp_ref[...]).astype(BF16)
    acc = h1
    yield
    for c in range(MLP_HIDDEN // D_MODEL):
        cs = slice(c * D_MODEL, (c + 1) * D_MODEL)
        up = jnp.maximum(jnp.dot(xn, wup_ref[:, cs], preferred_element_type=F32), 0.0)
        yield
        acc = acc + jnp.dot((up * up).astype(BF16), wdown_ref[cs, :], preferred_element_type=F32)
        yield
    if final_norm:
        acc = _rms(acc, nfin_ref[...])
    out_ref[...] = acc


GDN_PHASE_ORDER = "rmrmrmrmrmrmrmrmrmrmr"
HGRN_PHASE_ORDER = "rmmrmmrmmrmmrmrm"


def _rec_out_kernel(*refs, gdn, n_chunks, tiles_per_seq, n_tiles, final_norm):
    rec_refs = refs[:5]
    (sg_ref, h_ref, mw_ref, wout_ref, nmlp_ref, wup_ref, wdown_ref, nfin_ref,
     out_ref, s_ref, o_buf) = refs[5:]
    step = pl.program_id(0)

    @pl.when(step == 0)
    def _():
        o_buf[...] = jnp.zeros_like(o_buf)

    slot = step % 2
    tile = jnp.minimum(step, n_tiles - 1)
    rec = _gdn_rec_tile if gdn else _hgrn_rec_tile
    stages = {
        "r": rec(*rec_refs, o_buf.at[slot], s_ref, n_chunks=n_chunks,
                 first_tile=tile % tiles_per_seq == 0),
        "m": _out_mlp_tile(o_buf.at[1 - slot], sg_ref, h_ref, mw_ref, wout_ref, nmlp_ref,
                           wup_ref, wdown_ref, nfin_ref, out_ref, per_head_norm=gdn,
                           final_norm=final_norm),
    }
    done = set()
    for tag in GDN_PHASE_ORDER if gdn else HGRN_PHASE_ORDER:
        if next(stages[tag], "end") == "end":
            done.add(tag)
    assert done == {"r", "m"}, "phase order does not exhaust both stages"


def _rec_out(rec_inputs, gdn, sg, h, mix_norm_w, w_out, norm_mlp, w_up, w_down, norm_final,
             seq_len, final_norm):
    n = h.shape[0]
    rows = ROWS_REC
    n_tiles = n // rows
    cur_spec = pl.BlockSpec((HEADS, rows, HEAD_DIM),
                            lambda s: (0, jnp.minimum(s, n_tiles - 1), 0))
    prev_spec = pl.BlockSpec((rows, D_MODEL), lambda s: (jnp.maximum(s - 1, 0), 0))
    mw = mix_norm_w.reshape(1, -1).astype(F32)
    scratch = [pltpu.VMEM((HEADS, HEAD_DIM, HEAD_DIM), F32),
               pltpu.VMEM((2, HEADS, rows, HEAD_DIM), F32)]
    if gdn:
        rec_specs = [cur_spec] * 5
    else:
        rec_inputs = tuple(rec_inputs) + (_sub_block_column_selector(),)
        rec_specs = [cur_spec] * 4 + [_resident((SUB * HEAD_DIM, CHUNK))]
    return pl.pallas_call(
        functools.partial(_rec_out_kernel, gdn=gdn, n_chunks=rows // CHUNK,
                          tiles_per_seq=seq_len // rows, n_tiles=n_tiles, final_norm=final_norm),
        grid=(n_tiles + 1,),
        in_specs=rec_specs + [prev_spec, prev_spec,
                              _resident(mw.shape), _resident((D_MODEL, D_MODEL)),
                              _resident((1, D_MODEL)), _resident((D_MODEL, MLP_HIDDEN)),
                              _resident((MLP_HIDDEN, D_MODEL)), _resident((1, D_MODEL))],
        out_specs=prev_spec,
        out_shape=jax.ShapeDtypeStruct((n, D_MODEL), F32),
        scratch_shapes=scratch,
        compiler_params=_params(1),
        name="gdn_rec_out" if gdn else "hgrn_rec_out",
    )(*rec_inputs, sg, h, mw, w_out.astype(BF16), norm_mlp.reshape(1, D_MODEL),
      w_up.astype(BF16), w_down.astype(BF16), norm_final.reshape(1, D_MODEL))


def kernel(x, gdn_w_in, gdn_conv, gdn_a_log, gdn_dt_bias, gdn_onorm, gdn_w_out, hgrn_w_in, hgrn_lb_logits, hgrn_gnorm, hgrn_w_out, norm_mix, norm_mlp, mlp_w_up, mlp_w_down, norm_final):
    batch, seq_len, d_model = x.shape
    depth = norm_mix.shape[0]
    assert d_model == D_MODEL and seq_len % max(ROWS_IN, ROWS_REC) == 0
    h = x.reshape(batch * seq_len, d_model)
    for i in range(depth):
        j = i // 2
        gdn = i % 2 == 0
        if gdn:
            q, k, v, sg, g, beta = _gdn_in(h, norm_mix[i], gdn_w_in[j], gdn_conv[j],
                                           gdn_a_log[j], gdn_dt_bias[j], seq_len)
            rec_inputs = (q, k, v, g, beta)
            mix_norm_w, w_out = gdn_onorm[j], gdn_w_out[j]
        else:
            q, k, v, g, sg = _hgrn_in(h, norm_mix[i], hgrn_w_in[j], hgrn_lb_logits, i)
            rec_inputs = (q, k, v, g)
            mix_norm_w, w_out = hgrn_gnorm[j], hgrn_w_out[j]
        h = _rec_out(rec_inputs, gdn, sg, h, mix_norm_w, w_out, norm_mlp[i], mlp_w_up[i],
                     mlp_w_down[i], norm_final, seq_len, final_norm=(i == depth - 1))
    return h.reshape(batch, seq_len, d_model)
```

```python
import functools

import numpy as np
import jax
import jax.numpy as jnp
from jax import lax
from jax.experimental import pallas as pl
from jax.experimental.pallas import tpu as pltpu

F32 = jnp.float32
BF16 = jnp.bfloat16

D_MODEL = 1024
HEADS = 8
HEAD_DIM = 128
CHUNK = 64
SUB = 16
N_SUB = CHUNK // SUB
CONV_K = 4
MLP_HIDDEN = 4 * D_MODEL
EPS = 1e-6
LOG2_E = 1.4426950408889634
LANES = 128
HALO_ROWS = 8
COL_GROUP = 256
HEADS_PER_GROUP = COL_GROUP // HEAD_DIM
GDN_PROJ_WIDTH = 4 * D_MODEL + LANES
V7X_VMEM_LIMIT_BYTES = 56 * 1024 * 1024

ROWS_IN = 512
ROWS_REC = 256


def _resident(shape):
    nd = len(shape)
    return pl.BlockSpec(shape, lambda *_: (0,) * nd, pipeline_mode=pl.Buffered(1))


def _params(n_axes):
    return pltpu.CompilerParams(
        dimension_semantics=("arbitrary",) * n_axes,
        vmem_limit_bytes=V7X_VMEM_LIMIT_BYTES)


def _dot(a, b):
    return jnp.dot(a.astype(BF16), b.astype(BF16), preferred_element_type=F32)


def _dot_nt(a, b):
    return lax.dot_general(a.astype(BF16), b.astype(BF16),
                           (((1,), (1,)), ((), ())), preferred_element_type=F32)


def _dot_tn(a, b):
    return lax.dot_general(a.astype(BF16), b.astype(BF16),
                           (((0,), (0,)), ((), ())), preferred_element_type=F32)


def _split2(x):
    hi = x.astype(BF16)
    lo = (x - hi.astype(F32)).astype(BF16)
    return hi, lo


def _dot_split(a_split, b_split):
    a_hi, a_lo = a_split
    b_hi, b_lo = b_split
    n = b_hi.shape[1]
    rhs = jnp.concatenate([jnp.concatenate([b_hi, b_lo], axis=1),
                           jnp.concatenate([b_hi, jnp.zeros_like(b_lo)], axis=1)], axis=0)
    out = jnp.dot(jnp.concatenate([a_hi, a_lo], axis=1), rhs, preferred_element_type=F32)
    return out[:, :n] + out[:, n:]


def _tril3():
    row = lax.broadcasted_iota(jnp.int32, (CHUNK, 3 * CHUNK), 0)
    col = lax.broadcasted_iota(jnp.int32, (CHUNK, 3 * CHUNK), 1)
    return jnp.where(row >= (col & (CHUNK - 1)), 1.0, 0.0).astype(BF16)


def _cumsum_heads(tril3, tiles):
    x = jnp.concatenate(tiles, axis=1)
    x1 = x.astype(BF16)
    r1 = x - x1.astype(F32)
    x2 = r1.astype(BF16)
    x3 = (r1 - x2.astype(F32)).astype(BF16)
    out = jnp.dot(tril3, jnp.concatenate([x1, x2, x3], axis=0), preferred_element_type=F32)
    return [out[:, h * LANES:(h + 1) * LANES] for h in range(len(tiles))]


def _rms(x, w):
    ms = jnp.mean(x * x, axis=-1, keepdims=True)
    return x * lax.rsqrt(ms + EPS) * w


def _sigmoid(x):
    return 1.0 / (1.0 + jnp.exp(-x))


def _silu(x, scale=1.0):
    half = x * 0.5
    hs = half if scale == 1.0 else x * (0.5 * scale)
    return hs + hs * jnp.tanh(half)


def _softplus(x):
    return jnp.maximum(x, 0.0) + jnp.log(1.0 + jnp.exp(-jnp.abs(x)))


def _gdn_in_kernel(h_ref, nw_ref, w_ref, cw_ref, alog_ref, dtb_ref,
                   q_ref, k_ref, v_ref, sg_ref, g_ref, b_ref,
                   pbuf, halo, *, rows, tiles_per_seq):
    @pl.when(pl.program_id(0) % tiles_per_seq == 0)
    def _():
        halo[...] = jnp.zeros_like(halo)

    xn = _rms(h_ref[...], nw_ref[...]).astype(BF16)

    for ci, out_ref in enumerate((q_ref, k_ref, v_ref)):
        for grp in range(D_MODEL // COL_GROUP):
            cs = slice(ci * D_MODEL + grp * COL_GROUP, ci * D_MODEL + (grp + 1) * COL_GROUP)
            p = jnp.dot(xn, w_ref[:, cs], preferred_element_type=F32)
            pbuf[0:HALO_ROWS, cs] = halo[:, cs]
            pbuf[HALO_ROWS:HALO_ROWS + rows, cs] = p
            halo[:, cs] = p[rows - HALO_ROWS:rows, :]
            acc = cw_ref[CONV_K - 1:CONV_K, cs] * p
            for tap in range(CONV_K - 1):
                start = HALO_ROWS - (CONV_K - 1) + tap
                acc = acc + cw_ref[tap:tap + 1, cs] * pbuf[start:start + rows, cs]
            s = _silu(acc)
            for hh in range(HEADS_PER_GROUP):
                sl = s[:, hh * HEAD_DIM:(hh + 1) * HEAD_DIM]
                if ci < 2:
                    ss = jnp.sum(sl * sl, axis=-1, keepdims=True)
                    sl = sl * (lax.rsqrt(ss + EPS) * (HEAD_DIM ** -0.5 if ci == 0 else 1.0))
                out_ref[grp * HEADS_PER_GROUP + hh] = sl

    for grp in range(D_MODEL // COL_GROUP):
        cs = slice(grp * COL_GROUP, (grp + 1) * COL_GROUP)
        pg = jnp.dot(xn, w_ref[:, 3 * D_MODEL + cs.start:3 * D_MODEL + cs.stop],
                     preferred_element_type=F32)
        sg_ref[:, cs] = _silu(pg)

    pab = jnp.dot(xn, w_ref[:, 4 * D_MODEL:GDN_PROJ_WIDTH], preferred_element_type=F32)
    gv = -jnp.exp(alog_ref[...]) * _softplus(pab + dtb_ref[...])
    bv = _sigmoid(pab)
    for h in range(HEADS):
        g_ref[h] = jnp.broadcast_to(gv[:, h:h + 1], (rows, LANES))
        b_ref[h] = jnp.broadcast_to(bv[:, HEADS + h:HEADS + h + 1], (rows, LANES))


def _gdn_in(h, norm_w, w_in, conv_w, a_log, dt_bias, seq_len):
    n = h.shape[0]
    rows = ROWS_IN
    w = jnp.pad(w_in, ((0, 0), (0, GDN_PROJ_WIDTH - w_in.shape[1]))).astype(BF16)
    alog = jnp.pad(a_log.astype(F32), (0, LANES - HEADS)).reshape(1, LANES)
    dtb = jnp.pad(dt_bias.astype(F32), (0, LANES - HEADS)).reshape(1, LANES)
    head_major = jax.ShapeDtypeStruct((HEADS, n, HEAD_DIM), F32)
    hm_spec = pl.BlockSpec((HEADS, rows, HEAD_DIM), lambda i: (0, i, 0))
    row_spec = pl.BlockSpec((rows, D_MODEL), lambda i: (i, 0))
    return pl.pallas_call(
        functools.partial(_gdn_in_kernel, rows=rows, tiles_per_seq=seq_len // rows),
        grid=(n // rows,),
        in_specs=[row_spec, _resident((1, D_MODEL)), _resident((D_MODEL, GDN_PROJ_WIDTH)),
                  _resident((CONV_K, 3 * D_MODEL)), _resident((1, LANES)), _resident((1, LANES))],
        out_specs=[hm_spec, hm_spec, hm_spec, row_spec, hm_spec, hm_spec],
        out_shape=[head_major, head_major, head_major,
                   jax.ShapeDtypeStruct((n, D_MODEL), F32), head_major, head_major],
        scratch_shapes=[pltpu.VMEM((HALO_ROWS + rows, 3 * D_MODEL), F32),
                        pltpu.VMEM((HALO_ROWS, 3 * D_MODEL), F32)],
        compiler_params=_params(1),
        name="gdn_in",
    )(h, norm_w.reshape(1, D_MODEL), w, conv_w, alog, dtb)


def _gdn_rec_tile(q_ref, k_ref, v_ref, g_ref, b_ref, o_ref, s_ref, *, n_chunks, first_tile):
    @pl.when(first_tile)
    def _():
        s_ref[...] = jnp.zeros_like(s_ref)

    pair = 2 * CHUNK
    n_pairs = HEADS // 2
    tril3 = _tril3()
    row = lax.broadcasted_iota(jnp.int32, (pair, pair), 0)
    col = lax.broadcasted_iota(jnp.int32, (pair, pair), 1)
    first_head = row < CHUNK
    same_head = first_head == (col < CHUNK)
    causal = same_head & (row >= col)
    strict = same_head & (row > col)
    eye = jnp.where(row == col, 1.0, 0.0).astype(F32)

    def halves(x):
        return x[:CHUNK], x[CHUNK:]

    items = [(c, p) for c in range(n_chunks) for p in range(n_pairs)]

    def load_pairs(ref):
        return [jnp.concatenate([ref[2 * p, c * CHUNK:(c + 1) * CHUNK, :],
                                 ref[2 * p + 1, c * CHUNK:(c + 1) * CHUNK, :]], axis=0)
                for c, p in items]

    q, k, v, beta = (load_pairs(r) for r in (q_ref, k_ref, v_ref, b_ref))
    gc_h = _cumsum_heads(tril3,[g_ref[h, c * CHUNK:(c + 1) * CHUNK, :]
                                       for c in range(n_chunks) for h in range(HEADS)])
    gc = [jnp.concatenate([gc_h[c * HEADS + 2 * p], gc_h[c * HEADS + 2 * p + 1]], axis=0)
          for c, p in items]
    decay = [jnp.exp(jnp.where(causal, x - x.T, -jnp.inf)) for x in gc]
    kb = [a * b for a, b in zip(k, beta)]
    kk_qk = [_dot_nt(b, jnp.concatenate([a, c_], axis=0)) for a, b, c_ in zip(kb, k, q)]
    lmat = [jnp.where(strict, x[:, :pair].T * d, 0.0) for x, d in zip(kk_qk, decay)]
    a_qk = [jnp.where(causal, x[:, pair:].T * d, 0.0) for x, d in zip(kk_qk, decay)]
    egc = [jnp.exp(x) for x in gc]
    rhs = [jnp.concatenate([a * b, kb_ * e], axis=1) for a, b, kb_, e in zip(v, beta, kb, egc)]
    pw_s = [_split2(-x) for x in lmat]
    inv = [eye - x for x in lmat]
    yield
    for _ in range(5):
        pw = [_dot_split(s, s) for s in pw_s]
        pw_s = [_split2(x) for x in pw]
        inv = [i + _dot_split(_split2(i), s) for i, s in zip(inv, pw_s)]
        yield
    sol = [_dot_split(_split2(i), _split2(r)) for i, r in zip(inv, rhs)]
    qe = [a * e for a, e in zip(q, egc)]
    g_last = [jnp.where(first_head, x[CHUNK - 1:CHUNK, :], x[pair - 1:pair, :]) for x in gc]
    k_dec = [a * jnp.exp(gl - x) for a, gl, x in zip(k, g_last, gc)]
    yield

    state = [s_ref[h] for h in range(HEADS)]
    for c in range(n_chunks):
        ps = range(c * n_pairs, (c + 1) * n_pairs)
        w_h = [t for i in ps for t in halves(sol[i][:, HEAD_DIM:])]
        qe_h = [t for i in ps for t in halves(qe[i])]
        wq_s = [_dot(jnp.concatenate([a, b], axis=0), s) for a, b, s in zip(w_h, qe_h, state)]
        w_s = [x[:CHUNK] for x in wq_s]
        q_s = [x[CHUNK:] for x in wq_s]
        v_new = [sol[i][:, :HEAD_DIM] - jnp.concatenate([w_s[2 * p], w_s[2 * p + 1]], axis=0)
                 for p, i in enumerate(ps)]
        o = [jnp.concatenate([q_s[2 * p], q_s[2 * p + 1]], axis=0) + _dot(a_qk[i], v_new[p])
             for p, i in enumerate(ps)]
        for p in range(n_pairs):
            o_ref[2 * p, c * CHUNK:(c + 1) * CHUNK, :] = o[p][:CHUNK]
            o_ref[2 * p + 1, c * CHUNK:(c + 1) * CHUNK, :] = o[p][CHUNK:]
        upd = [_dot_tn(a, b) for a, b in zip([t for i in ps for t in halves(k_dec[i])],
                                             [t for x in v_new for t in halves(x)])]
        state = [s * jnp.exp(gc_h[c * HEADS + h][CHUNK - 1:CHUNK, :]) + d
                 for h, (s, d) in enumerate(zip(state, upd))]
        if c + 1 < n_chunks:
            yield
    for h in range(HEADS):
        s_ref[h] = state[h]


def _hgrn_in_kernel(h_ref, nw_ref, w_ref, lg_ref,
                    q_ref, k_ref, v_ref, g_ref, sg_ref, *, layer):
    lg = lg_ref[...]
    m = jnp.max(lg, axis=0, keepdims=True)
    e = jnp.exp(lg - m)
    sm = e / jnp.sum(e, axis=0, keepdims=True)
    cs = sm[0:1, :]
    for r in range(1, layer + 1):
        cs = cs + sm[r:r + 1, :]
    lb = cs - sm[0:1, :]
    log_lb = jnp.log(lb)
    log_1m_lb = jnp.log1p(-lb)

    xn = _rms(h_ref[...], nw_ref[...]).astype(BF16)

    def proj(i, cs):
        return jnp.dot(xn, w_ref[:, i * D_MODEL + cs.start:i * D_MODEL + cs.stop],
                       preferred_element_type=F32)

    def heads_out(ref, grp, val):
        for hh in range(HEADS_PER_GROUP):
            ref[grp * HEADS_PER_GROUP + hh] = val[:, hh * HEAD_DIM:(hh + 1) * HEAD_DIM]

    def finish(kind, grp, p):
        cs = slice(grp * COL_GROUP, (grp + 1) * COL_GROUP)
        if kind == 0:
            heads_out(q_ref, grp, _silu(p, HEAD_DIM ** -0.5))
        elif kind == 1:
            t = jnp.exp(-jnp.abs(p))
            log_sig = jnp.minimum(p, 0.0) - jnp.log(1.0 + t)
            a = log_lb[:, cs]
            b = log_1m_lb[:, cs] + log_sig
            heads_out(g_ref, grp, jnp.maximum(a, b) + jnp.log(1.0 + jnp.exp(-jnp.abs(a - b))))
            inv = 1.0 / (1.0 + t)
            heads_out(k_ref, grp, (1.0 - lb[:, cs]) * jnp.where(p >= 0.0, t * inv, inv))
        elif kind == 2:
            heads_out(v_ref, grp, p)
        else:
            sg_ref[:, cs] = _silu(p)

    for grp in range(D_MODEL // COL_GROUP):
        for kind in (1, 0, 3, 2):
            finish(kind, grp, proj(kind, slice(grp * COL_GROUP, (grp + 1) * COL_GROUP)))


def _hgrn_in(h, norm_w, w_in, lb_logits, layer):
    n = h.shape[0]
    rows = ROWS_IN
    depth = lb_logits.shape[0]
    head_major = jax.ShapeDtypeStruct((HEADS, n, HEAD_DIM), F32)
    hm_spec = pl.BlockSpec((HEADS, rows, HEAD_DIM), lambda i: (0, i, 0))
    row_spec = pl.BlockSpec((rows, D_MODEL), lambda i: (i, 0))
    return pl.pallas_call(
        functools.partial(_hgrn_in_kernel, layer=layer),
        grid=(n // rows,),
        in_specs=[row_spec, _resident((1, D_MODEL)), _resident((D_MODEL, 4 * D_MODEL)),
                  _resident((depth, D_MODEL))],
        out_specs=[hm_spec, hm_spec, hm_spec, hm_spec, row_spec],
        out_shape=[head_major] * 4 + [jax.ShapeDtypeStruct((n, D_MODEL), F32)],
        compiler_params=_params(1),
        name="hgrn_in",
    )(h, norm_w.reshape(1, D_MODEL), w_in.astype(BF16), lb_logits.astype(F32))


def _hgrn_rec_tile(q_ref, k_ref, v_ref, g_ref, ecol_ref, o_ref, s_ref, *, n_chunks, first_tile):
    @pl.when(first_tile)
    def _():
        s_ref[...] = jnp.zeros_like(s_ref)

    tril3 = _tril3()
    row = lax.broadcasted_iota(jnp.int32, (CHUNK, CHUNK), 0)
    col = lax.broadcasted_iota(jnp.int32, (CHUNK, CHUNK), 1)
    same_sub = ((row // SUB) == (col // SUB)) & (row >= col)
    sub_i = lax.broadcasted_iota(jnp.int32, (N_SUB, SUB, HEAD_DIM), 1)
    heads = range(HEADS)

    def diag_products(q, k, gc):
        g3 = (gc * LOG2_E).reshape(N_SUB, SUB, HEAD_DIM)
        q3 = q.reshape(N_SUB, SUB, HEAD_DIM)
        k3 = k.reshape(N_SUB, SUB, HEAD_DIM)
        half = SUB // 2
        zeros = jnp.zeros((N_SUB, half, HEAD_DIM), F32)
        pieces = []
        for j in range(SUB):
            lo = 0 if j < half else half
            dec = jnp.exp2(jnp.minimum(g3[:, lo:, :] - g3[:, j:j + 1, :], 0.0))
            piece = q3[:, lo:, :] * dec * k3[:, j:j + 1, :]
            if lo:
                piece = jnp.concatenate([zeros, piece], axis=1)
            pieces.append(piece.reshape(CHUNK, HEAD_DIM).astype(BF16))
        return jnp.concatenate(pieces, axis=1)

    items = [(c, h) for c in range(n_chunks) for h in heads]

    def tiles(ref):
        return [ref[h, c * CHUNK:(c + 1) * CHUNK, :] for c, h in items]

    q, k, v, g = (tiles(r) for r in (q_ref, k_ref, v_ref, g_ref))
    gc = _cumsum_heads(tril3, g)

    gb = [x - y for x, y in zip(gc, g)]
    a_rows = [[jnp.zeros((SUB, CHUNK), F32)] for _ in items]
    for s in range(1, N_SUB):
        lo, hi = s * SUB, (s + 1) * SUB
        for i in range(len(items)):
            gb_s = gb[i][lo:lo + 1, :]
            q_off = q[i][lo:hi, :] * jnp.exp(gc[i][lo:hi, :] - gb_s)
            k_off = jnp.concatenate(
                [k[i][:lo, :] * jnp.exp(gb_s - gc[i][:lo, :]),
                 jnp.zeros((CHUNK - lo, HEAD_DIM), F32)], axis=0)
            a_rows[i].append(_dot_nt(q_off, k_off))
    qe = [a * jnp.exp(x) for a, x in zip(q, gc)]
    g_last = [x[CHUNK - 1:CHUNK, :] for x in gc]
    upd = [_dot_tn(a, b * jnp.exp(gl - x)) for a, b, gl, x in zip(v, k, g_last, gc)]
    yield

    o_intra = []
    for c in range(n_chunks):
        ids = range(c * HEADS, (c + 1) * HEADS)
        prods = jnp.concatenate([diag_products(q[i], k[i], gc[i]) for i in ids], axis=0)
        a_all = jnp.dot(prods, ecol_ref[...], preferred_element_type=F32)
        a_mat = [jnp.where(same_sub, a_all[h * CHUNK:(h + 1) * CHUNK], 0.0)
                 + jnp.concatenate(a_rows[i], axis=0) for h, i in enumerate(ids)]
        o_intra += [_dot(a, v[i]) for a, i in zip(a_mat, ids)]
        yield

    state_t = [s_ref[h] for h in heads]
    for c in range(n_chunks):
        ids = range(c * HEADS, (c + 1) * HEADS)
        o_state = [_dot_nt(qe[i], s) for i, s in zip(ids, state_t)]
        for h, i in enumerate(ids):
            o_ref[h, c * CHUNK:(c + 1) * CHUNK, :] = o_intra[i] + o_state[h]
        state_t = [s * jnp.exp(g_last[i]) + upd[i] for i, s in zip(ids, state_t)]
    for h in heads:
        s_ref[h] = state_t[h]


def _sub_block_column_selector():
    j = np.arange(SUB * HEAD_DIM) // HEAD_DIM
    c = np.arange(CHUNK) % SUB
    return jnp.asarray(j[:, None] == c[None, :], dtype=BF16)


def _out_mlp_tile(o_ref, sg_ref, h_ref, mw_ref, wout_ref, nmlp_ref, wup_ref, wdown_ref,
                  nfin_ref, out_ref, *, per_head_norm, final_norm):
    if per_head_norm:
        y = jnp.concatenate([_rms(o_ref[h], mw_ref[...]) for h in range(HEADS)], axis=1)
    else:
        y = _rms(jnp.concatenate([o_ref[h] for h in range(HEADS)], axis=1), mw_ref[...])
    y = y * sg_ref[...]
    h1 = h_ref[...] + jnp.dot(y.astype(BF16), wout_ref[...], preferred_element_type=F32)
    xn = _rms(h1, nmlp_ref[...]).astype(BF16)
    acc = h1
    yield
    for c in range(MLP_HIDDEN // D_MODEL):
        cs = slice(c * D_MODEL, (c + 1) * D_MODEL)
        up = jnp.maximum(jnp.dot(xn, wup_ref[:, cs], preferred_element_type=F32), 0.0)
        yield
        acc = acc + jnp.dot((up * up).astype(BF16), wdown_ref[cs, :], preferred_element_type=F32)
        yield
    if final_norm:
        acc = _rms(acc, nfin_ref[...])
    out_ref[...] = acc


GDN_PHASE_ORDER = "rmrmrmrmrmrmrmrmrmrmr"
HGRN_PHASE_ORDER = "rmmrmmrmmrmmrmrm"


def _rec_out_kernel(*refs, gdn, n_chunks, tiles_per_seq, n_tiles, final_norm):
    rec_refs = refs[:5]
    (sg_ref, h_ref, mw_ref, wout_ref, nmlp_ref, wup_ref, wdown_ref, nfin_ref,
     out_ref, s_ref, o_buf) = refs[5:]
    step = pl.program_id(0)

    @pl.when(step == 0)
    def _():
        o_buf[...] = jnp.zeros_like(o_buf)

    slot = step % 2
    tile = jnp.minimum(step, n_tiles - 1)
    rec = _gdn_rec_tile if gdn else _hgrn_rec_tile
    stages = {
        "r": rec(*rec_refs, o_buf.at[slot], s_ref, n_chunks=n_chunks,
                 first_tile=tile % tiles_per_seq == 0),
        "m": _out_mlp_tile(o_buf.at[1 - slot], sg_ref, h_ref, mw_ref, wout_ref, nmlp_ref,
                           wup_ref, wdown_ref, nfin_ref, out_ref, per_head_norm=gdn,
                           final_norm=final_norm),
    }
    done = set()
    for tag in GDN_PHASE_ORDER if gdn else HGRN_PHASE_ORDER:
        if next(stages[tag], "end") == "end":
            done.add(tag)
    assert done == {"r", "m"}, "phase order does not exhaust both stages"


def _rec_out(rec_inputs, gdn, sg, h, mix_norm_w, w_out, norm_mlp, w_up, w_down, norm_final,
             seq_len, final_norm):
    n = h.shape[0]
    rows = ROWS_REC
    n_tiles = n // rows
    cur_spec = pl.BlockSpec((HEADS, rows, HEAD_DIM),
                            lambda s: (0, jnp.minimum(s, n_tiles - 1), 0))
    prev_spec = pl.BlockSpec((rows, D_MODEL), lambda s: (jnp.maximum(s - 1, 0), 0))
    mw = mix_norm_w.reshape(1, -1).astype(F32)
    scratch = [pltpu.VMEM((HEADS, HEAD_DIM, HEAD_DIM), F32),
               pltpu.VMEM((2, HEADS, rows, HEAD_DIM), F32)]
    if gdn:
        rec_specs = [cur_spec] * 5
    else:
        rec_inputs = tuple(rec_inputs) + (_sub_block_column_selector(),)
        rec_specs = [cur_spec] * 4 + [_resident((SUB * HEAD_DIM, CHUNK))]
    return pl.pallas_call(
        functools.partial(_rec_out_kernel, gdn=gdn, n_chunks=rows // CHUNK,
                          tiles_per_seq=seq_len // rows, n_tiles=n_tiles, final_norm=final_norm),
        grid=(n_tiles + 1,),
        in_specs=rec_specs + [prev_spec, prev_spec,
                              _resident(mw.shape), _resident((D_MODEL, D_MODEL)),
                              _resident((1, D_MODEL)), _resident((D_MODEL, MLP_HIDDEN)),
                              _resident((MLP_HIDDEN, D_MODEL)), _resident((1, D_MODEL))],
        out_specs=prev_spec,
        out_shape=jax.ShapeDtypeStruct((n, D_MODEL), F32),
        scratch_shapes=scratch,
        compiler_params=_params(1),
        name="gdn_rec_out" if gdn else "hgrn_rec_out",
    )(*rec_inputs, sg, h, mw, w_out.astype(BF16), norm_mlp.reshape(1, D_MODEL),
      w_up.astype(BF16), w_down.astype(BF16), norm_final.reshape(1, D_MODEL))


def kernel(x, gdn_w_in, gdn_conv, gdn_a_log, gdn_dt_bias, gdn_onorm, gdn_w_out, hgrn_w_in, hgrn_lb_logits, hgrn_gnorm, hgrn_w_out, norm_mix, norm_mlp, mlp_w_up, mlp_w_down, norm_final):
    batch, seq_len, d_model = x.shape
    depth = norm_mix.shape[0]
    assert d_model == D_MODEL and seq_len % max(ROWS_IN, ROWS_REC) == 0
    h = x.reshape(batch * seq_len, d_model)
    for i in range(depth):
        j = i // 2
        gdn = i % 2 == 0
        if gdn:
            q, k, v, sg, g, beta = _gdn_in(h, norm_mix[i], gdn_w_in[j], gdn_conv[j],
                                           gdn_a_log[j], gdn_dt_bias[j], seq_len)
            rec_inputs = (q, k, v, g, beta)
            mix_norm_w, w_out = gdn_onorm[j], gdn_w_out[j]
        else:
            q, k, v, g, sg = _hgrn_in(h, norm_mix[i], hgrn_w_in[j], hgrn_lb_logits, i)
            rec_inputs = (q, k, v, g)
            mix_norm_w, w_out = hgrn_gnorm[j], hgrn_w_out[j]
        h = _rec_out(rec_inputs, gdn, sg, h, mix_norm_w, w_out, norm_mlp[i], mlp_w_up[i],
                     mlp_w_down[i], norm_final, seq_len, final_norm=(i == depth - 1))
    return h.reshape(batch, seq_len, d_model)
```

```python
import functools

import numpy as np
import jax
import jax.numpy as jnp
from jax import lax
from jax.experimental import pallas as pl
from jax.experimental.pallas import tpu as pltpu

F32 = jnp.float32
BF16 = jnp.bfloat16

D_MODEL = 1024
HEADS = 8
HEAD_DIM = 128
CHUNK = 64
SUB = 16
N_SUB = CHUNK // SUB
CONV_K = 4
MLP_HIDDEN = 4 * D_MODEL
EPS = 1e-6
LOG2_E = 1.4426950408889634
LANES = 128
HALO_ROWS = 8
COL_GROUP = 256
HEADS_PER_GROUP = COL_GROUP // HEAD_DIM
GDN_PROJ_WIDTH = 4 * D_MODEL + LANES
V7X_VMEM_LIMIT_BYTES = 56 * 1024 * 1024

ROWS_IN = 512
ROWS_REC = 256


def _resident(shape):
    nd = len(shape)
    return pl.BlockSpec(shape, lambda *_: (0,) * nd, pipeline_mode=pl.Buffered(1))


def _params(n_axes):
    return pltpu.CompilerParams(
        dimension_semantics=("arbitrary",) * n_axes,
        vmem_limit_bytes=V7X_VMEM_LIMIT_BYTES)


def _dot(a, b):
    return jnp.dot(a.astype(BF16), b.astype(BF16), preferred_element_type=F32)


def _dot_nt(a, b):
    return lax.dot_general(a.astype(BF16), b.astype(BF16),
                           (((1,), (1,)), ((), ())), preferred_element_type=F32)


def _dot_tn(a, b):
    return lax.dot_general(a.astype(BF16), b.astype(BF16),
                           (((0,), (0,)), ((), ())), preferred_element_type=F32)


def _split2(x):
    hi = x.astype(BF16)
    lo = (x - hi.astype(F32)).astype(BF16)
    return hi, lo


def _dot_split(a_split, b_split):
    a_hi, a_lo = a_split
    b_hi, b_lo = b_split
    n = b_hi.shape[1]
    rhs = jnp.concatenate([jnp.concatenate([b_hi, b_lo], axis=1),
                           jnp.concatenate([b_hi, jnp.zeros_like(b_lo)], axis=1)], axis=0)
    out = jnp.dot(jnp.concatenate([a_hi, a_lo], axis=1), rhs, preferred_element_type=F32)
    return out[:, :n] + out[:, n:]


def _tril3():
    row = lax.broadcasted_iota(jnp.int32, (CHUNK, 3 * CHUNK), 0)
    col = lax.broadcasted_iota(jnp.int32, (CHUNK, 3 * CHUNK), 1)
    return jnp.where(row >= (col & (CHUNK - 1)), 1.0, 0.0).astype(BF16)


def _cumsum_heads(tril3, tiles):
    x = jnp.concatenate(tiles, axis=1)
    x1 = x.astype(BF16)
    r1 = x - x1.astype(F32)
    x2 = r1.astype(BF16)
    x3 = (r1 - x2.astype(F32)).astype(BF16)
    out = jnp.dot(tril3, jnp.concatenate([x1, x2, x3], axis=0), preferred_element_type=F32)
    return [out[:, h * LANES:(h + 1) * LANES] for h in range(len(tiles))]


def _rms(x, w):
    ms = jnp.mean(x * x, axis=-1, keepdims=True)
    return x * lax.rsqrt(ms + EPS) * w


def _sigmoid(x):
    return 1.0 / (1.0 + jnp.exp(-x))


def _silu(x, scale=1.0):
    half = x * 0.5
    hs = half if scale == 1.0 else x * (0.5 * scale)
    return hs + hs * jnp.tanh(half)


def _softplus(x):
    return jnp.maximum(x, 0.0) + jnp.log(1.0 + jnp.exp(-jnp.abs(x)))


def _gdn_in_kernel(h_ref, nw_ref, w_ref, cw_ref, alog_ref, dtb_ref,
                   q_ref, k_ref, v_ref, sg_ref, g_ref, b_ref,
                   pbuf, halo, *, rows, tiles_per_seq):
    @pl.when(pl.program_id(0) % tiles_per_seq == 0)
    def _():
        halo[...] = jnp.zeros_like(halo)

    xn = _rms(h_ref[...], nw_ref[...]).astype(BF16)

    for ci, out_ref in enumerate((q_ref, k_ref, v_ref)):
        for grp in range(D_MODEL // COL_GROUP):
            cs = slice(ci * D_MODEL + grp * COL_GROUP, ci * D_MODEL + (grp + 1) * COL_GROUP)
            p = jnp.dot(xn, w_ref[:, cs], preferred_element_type=F32)
            for hh in range(HEADS_PER_GROUP):
                head = grp * HEADS_PER_GROUP + hh
                slab = ci * HEADS + head
                hs = slice(cs.start + hh * HEAD_DIM, cs.start + (hh + 1) * HEAD_DIM)
                ph = p[:, hh * HEAD_DIM:(hh + 1) * HEAD_DIM]
                pbuf[slab, 0:HALO_ROWS, :] = halo[slab]
                pbuf[slab, HALO_ROWS:HALO_ROWS + rows, :] = ph
                halo[slab] = ph[rows - HALO_ROWS:rows, :]
                acc = cw_ref[CONV_K - 1:CONV_K, hs] * ph
                for tap in range(CONV_K - 1):
                    start = HALO_ROWS - (CONV_K - 1) + tap
                    acc = acc + cw_ref[tap:tap + 1, hs] * pbuf[slab, start:start + rows, :]
                sl = _silu(acc)
                if ci < 2:
                    ss = jnp.sum(sl * sl, axis=-1, keepdims=True)
                    sl = sl * (lax.rsqrt(ss + EPS) * (HEAD_DIM ** -0.5 if ci == 0 else 1.0))
                out_ref[head] = sl

    for grp in range(D_MODEL // COL_GROUP):
        cs = slice(grp * COL_GROUP, (grp + 1) * COL_GROUP)
        pg = jnp.dot(xn, w_ref[:, 3 * D_MODEL + cs.start:3 * D_MODEL + cs.stop],
                     preferred_element_type=F32)
        sg_ref[:, cs] = _silu(pg)

    pab = jnp.dot(xn, w_ref[:, 4 * D_MODEL:GDN_PROJ_WIDTH], preferred_element_type=F32)
    gv = -jnp.exp(alog_ref[...]) * _softplus(pab + dtb_ref[...])
    bv = _sigmoid(pab)
    for h in range(HEADS):
        g_ref[h] = jnp.broadcast_to(gv[:, h:h + 1], (rows, LANES))
        b_ref[h] = jnp.broadcast_to(bv[:, HEADS + h:HEADS + h + 1], (rows, LANES))


def _gdn_in(h, norm_w, w_in, conv_w, a_log, dt_bias, seq_len):
    n = h.shape[0]
    rows = ROWS_IN
    w = jnp.pad(w_in, ((0, 0), (0, GDN_PROJ_WIDTH - w_in.shape[1]))).astype(BF16)
    alog = jnp.pad(a_log.astype(F32), (0, LANES - HEADS)).reshape(1, LANES)
    dtb = jnp.pad(dt_bias.astype(F32), (0, LANES - HEADS)).reshape(1, LANES)
    head_major = jax.ShapeDtypeStruct((HEADS, n, HEAD_DIM), F32)
    hm_spec = pl.BlockSpec((HEADS, rows, HEAD_DIM), lambda i: (0, i, 0))
    row_spec = pl.BlockSpec((rows, D_MODEL), lambda i: (i, 0))
    return pl.pallas_call(
        functools.partial(_gdn_in_kernel, rows=rows, tiles_per_seq=seq_len // rows),
        grid=(n // rows,),
        in_specs=[row_spec, _resident((1, D_MODEL)), _resident((D_MODEL, GDN_PROJ_WIDTH)),
                  _resident((CONV_K, 3 * D_MODEL)), _resident((1, LANES)), _resident((1, LANES))],
        out_specs=[hm_spec, hm_spec, hm_spec, row_spec, hm_spec, hm_spec],
        out_shape=[head_major, head_major, head_major,
                   jax.ShapeDtypeStruct((n, D_MODEL), F32), head_major, head_major],
        scratch_shapes=[pltpu.VMEM((3 * HEADS, HALO_ROWS + rows, HEAD_DIM), F32),
                        pltpu.VMEM((3 * HEADS, HALO_ROWS, HEAD_DIM), F32)],
        compiler_params=_params(1),
        name="gdn_in",
    )(h, norm_w.reshape(1, D_MODEL), w, conv_w, alog, dtb)


def _gdn_rec_tile(q_ref, k_ref, v_ref, g_ref, b_ref, o_ref, s_ref, *, n_chunks, first_tile):
    @pl.when(first_tile)
    def _():
        s_ref[...] = jnp.zeros_like(s_ref)

    pair = 2 * CHUNK
    n_pairs = HEADS // 2
    tril3 = _tril3()
    row = lax.broadcasted_iota(jnp.int32, (pair, pair), 0)
    col = lax.broadcasted_iota(jnp.int32, (pair, pair), 1)
    first_head = row < CHUNK
    same_head = first_head == (col < CHUNK)
    causal = same_head & (row >= col)
    strict = same_head & (row > col)
    eye = jnp.where(row == col, 1.0, 0.0).astype(F32)

    def halves(x):
        return x[:CHUNK], x[CHUNK:]

    items = [(c, p) for c in range(n_chunks) for p in range(n_pairs)]

    def load_pairs(ref):
        return [jnp.concatenate([ref[2 * p, c * CHUNK:(c + 1) * CHUNK, :],
                                 ref[2 * p + 1, c * CHUNK:(c + 1) * CHUNK, :]], axis=0)
                for c, p in items]

    q, k, v, beta = (load_pairs(r) for r in (q_ref, k_ref, v_ref, b_ref))
    gc_h = _cumsum_heads(tril3,[g_ref[h, c * CHUNK:(c + 1) * CHUNK, :]
                                       for c in range(n_chunks) for h in range(HEADS)])
    gc = [jnp.concatenate([gc_h[c * HEADS + 2 * p], gc_h[c * HEADS + 2 * p + 1]], axis=0)
          for c, p in items]
    decay = [jnp.exp(jnp.where(causal, x - x.T, -jnp.inf)) for x in gc]
    kb = [a * b for a, b in zip(k, beta)]
    kk_qk = [_dot_nt(b, jnp.concatenate([a, c_], axis=0)) for a, b, c_ in zip(kb, k, q)]
    lmat = [jnp.where(strict, x[:, :pair].T * d, 0.0) for x, d in zip(kk_qk, decay)]
    a_qk = [jnp.where(causal, x[:, pair:].T * d, 0.0) for x, d in zip(kk_qk, decay)]
    egc = [jnp.exp(x) for x in gc]
    rhs = [jnp.concatenate([a * b, kb_ * e], axis=1) for a, b, kb_, e in zip(v, beta, kb, egc)]
    pw_s = [_split2(-x) for x in lmat]
    inv = [eye - x for x in lmat]
    yield
    for _ in range(5):
        pw = [_dot_split(s, s) for s in pw_s]
        pw_s = [_split2(x) for x in pw]
        inv = [i + _dot_split(_split2(i), s) for i, s in zip(inv, pw_s)]
        yield
    sol = [_dot_split(_split2(i), _split2(r)) for i, r in zip(inv, rhs)]
    qe = [a * e for a, e in zip(q, egc)]
    g_last = [jnp.where(first_head, x[CHUNK - 1:CHUNK, :], x[pair - 1:pair, :]) for x in gc]
    k_dec = [a * jnp.exp(gl - x) for a, gl, x in zip(k, g_last, gc)]
    yield

    state = [s_ref[h] for h in range(HEADS)]
    for c in range(n_chunks):
        ps = range(c * n_pairs, (c + 1) * n_pairs)
        w_h = [t for i in ps for t in halves(sol[i][:, HEAD_DIM:])]
        qe_h = [t for i in ps for t in halves(qe[i])]
        wq_s = [_dot(jnp.concatenate([a, b], axis=0), s) for a, b, s in zip(w_h, qe_h, state)]
        w_s = [x[:CHUNK] for x in wq_s]
        q_s = [x[CHUNK:] for x in wq_s]
        v_new = [sol[i][:, :HEAD_DIM] - jnp.concatenate([w_s[2 * p], w_s[2 * p + 1]], axis=0)
                 for p, i in enumerate(ps)]
        o = [jnp.concatenate([q_s[2 * p], q_s[2 * p + 1]], axis=0) + _dot(a_qk[i], v_new[p])
             for p, i in enumerate(ps)]
        for p in range(n_pairs):
            o_ref[2 * p, c * CHUNK:(c + 1) * CHUNK, :] = o[p][:CHUNK]
            o_ref[2 * p + 1, c * CHUNK:(c + 1) * CHUNK, :] = o[p][CHUNK:]
        upd = [_dot_tn(a, b) for a, b in zip([t for i in ps for t in halves(k_dec[i])],
                                             [t for x in v_new for t in halves(x)])]
        state = [s * jnp.exp(gc_h[c * HEADS + h][CHUNK - 1:CHUNK, :]) + d
                 for h, (s, d) in enumerate(zip(state, upd))]
        if c + 1 < n_chunks:
            yield
    for h in range(HEADS):
        s_ref[h] = state[h]


def _hgrn_in_kernel(h_ref, nw_ref, w_ref, lg_ref,
                    q_ref, k_ref, v_ref, g_ref, sg_ref, *, layer):
    lg = lg_ref[...]
    m = jnp.max(lg, axis=0, keepdims=True)
    e = jnp.exp(lg - m)
    sm = e / jnp.sum(e, axis=0, keepdims=True)
    cs = sm[0:1, :]
    for r in range(1, layer + 1):
        cs = cs + sm[r:r + 1, :]
    lb = cs - sm[0:1, :]
    log_lb = jnp.log(lb)
    log_1m_lb = jnp.log1p(-lb)

    xn = _rms(h_ref[...], nw_ref[...]).astype(BF16)

    def proj(i, cs):
        return jnp.dot(xn, w_ref[:, i * D_MODEL + cs.start:i * D_MODEL + cs.stop],
                       preferred_element_type=F32)

    def heads_out(ref, grp, val):
        for hh in range(HEADS_PER_GROUP):
            ref[grp * HEADS_PER_GROUP + hh] = val[:, hh * HEAD_DIM:(hh + 1) * HEAD_DIM]

    def finish(kind, grp, p):
        cs = slice(grp * COL_GROUP, (grp + 1) * COL_GROUP)
        if kind == 0:
            heads_out(q_ref, grp, _silu(p, HEAD_DIM ** -0.5))
        elif kind == 1:
            t = jnp.exp(-jnp.abs(p))
            log_sig = jnp.minimum(p, 0.0) - jnp.log(1.0 + t)
            a = log_lb[:, cs]
            b = log_1m_lb[:, cs] + log_sig
            heads_out(g_ref, grp, jnp.maximum(a, b) + jnp.log(1.0 + jnp.exp(-jnp.abs(a - b))))
            inv = 1.0 / (1.0 + t)
            heads_out(k_ref, grp, (1.0 - lb[:, cs]) * jnp.where(p >= 0.0, t * inv, inv))
        elif kind == 2:
            heads_out(v_ref, grp, p)
        else:
            sg_ref[:, cs] = _silu(p)

    for grp in range(D_MODEL // COL_GROUP):
        for kind in (1, 0, 3, 2):
            finish(kind, grp, proj(kind, slice(grp * COL_GROUP, (grp + 1) * COL_GROUP)))


def _hgrn_in(h, norm_w, w_in, lb_logits, layer):
    n = h.shape[0]
    rows = ROWS_IN
    depth = lb_logits.shape[0]
    head_major = jax.ShapeDtypeStruct((HEADS, n, HEAD_DIM), F32)
    hm_spec = pl.BlockSpec((HEADS, rows, HEAD_DIM), lambda i: (0, i, 0))
    row_spec = pl.BlockSpec((rows, D_MODEL), lambda i: (i, 0))
    return pl.pallas_call(
        functools.partial(_hgrn_in_kernel, layer=layer),
        grid=(n // rows,),
        in_specs=[row_spec, _resident((1, D_MODEL)), _resident((D_MODEL, 4 * D_MODEL)),
                  _resident((depth, D_MODEL))],
        out_specs=[hm_spec, hm_spec, hm_spec, hm_spec, row_spec],
        out_shape=[head_major] * 4 + [jax.ShapeDtypeStruct((n, D_MODEL), F32)],
        compiler_params=_params(1),
        name="hgrn_in",
    )(h, norm_w.reshape(1, D_MODEL), w_in.astype(BF16), lb_logits.astype(F32))


def _hgrn_rec_tile(q_ref, k_ref, v_ref, g_ref, ecol_ref, o_ref, s_ref, *, n_chunks, first_tile):
    @pl.when(first_tile)
    def _():
        s_ref[...] = jnp.zeros_like(s_ref)

    tril3 = _tril3()
    row = lax.broadcasted_iota(jnp.int32, (CHUNK, CHUNK), 0)
    col = lax.broadcasted_iota(jnp.int32, (CHUNK, CHUNK), 1)
    same_sub = ((row // SUB) == (col // SUB)) & (row >= col)
    sub_i = lax.broadcasted_iota(jnp.int32, (N_SUB, SUB, HEAD_DIM), 1)
    heads = range(HEADS)

    def diag_products(q, k, gc):
        g3 = (gc * LOG2_E).reshape(N_SUB, SUB, HEAD_DIM)
        q3 = q.reshape(N_SUB, SUB, HEAD_DIM)
        k3 = k.reshape(N_SUB, SUB, HEAD_DIM)
        half = SUB // 2
        zeros = jnp.zeros((N_SUB, half, HEAD_DIM), F32)
        pieces = []
        for j in range(SUB):
            lo = 0 if j < half else half
            dec = jnp.exp2(jnp.minimum(g3[:, lo:, :] - g3[:, j:j + 1, :], 0.0))
            piece = q3[:, lo:, :] * dec * k3[:, j:j + 1, :]
            if lo:
                piece = jnp.concatenate([zeros, piece], axis=1)
            pieces.append(piece.reshape(CHUNK, HEAD_DIM).astype(BF16))
        return jnp.concatenate(pieces, axis=1)

    items = [(c, h) for c in range(n_chunks) for h in heads]

    def tiles(ref):
        return [ref[h, c * CHUNK:(c + 1) * CHUNK, :] for c, h in items]

    q, k, v, g = (tiles(r) for r in (q_ref, k_ref, v_ref, g_ref))
    gc = _cumsum_heads(tril3, g)

    gb = [x - y for x, y in zip(gc, g)]
    a_rows = [[jnp.zeros((SUB, CHUNK), F32)] for _ in items]
    for s in range(1, N_SUB):
        lo, hi = s * SUB, (s + 1) * SUB
        for i in range(len(items)):
            gb_s = gb[i][lo:lo + 1, :]
            q_off = q[i][lo:hi, :] * jnp.exp(gc[i][lo:hi, :] - gb_s)
            k_off = jnp.concatenate(
                [k[i][:lo, :] * jnp.exp(gb_s - gc[i][:lo, :]),
                 jnp.zeros((CHUNK - lo, HEAD_DIM), F32)], axis=0)
            a_rows[i].append(_dot_nt(q_off, k_off))
    qe = [a * jnp.exp(x) for a, x in zip(q, gc)]
    g_last = [x[CHUNK - 1:CHUNK, :] for x in gc]
    upd = [_dot_tn(a, b * jnp.exp(gl - x)) for a, b, gl, x in zip(v, k, g_last, gc)]
    yield

    o_intra = []
    for c in range(n_chunks):
        ids = range(c * HEADS, (c + 1) * HEADS)
        prods = jnp.concatenate([diag_products(q[i], k[i], gc[i]) for i in ids], axis=0)
        a_all = jnp.dot(prods, ecol_ref[...], preferred_element_type=F32)
        a_mat = [jnp.where(same_sub, a_all[h * CHUNK:(h + 1) * CHUNK], 0.0)
                 + jnp.concatenate(a_rows[i], axis=0) for h, i in enumerate(ids)]
        o_intra += [_dot(a, v[i]) for a, i in zip(a_mat, ids)]
        yield

    state_t = [s_ref[h] for h in heads]
    for c in range(n_chunks):
        ids = range(c * HEADS, (c + 1) * HEADS)
        o_state = [_dot_nt(qe[i], s) for i, s in zip(ids, state_t)]
        for h, i in enumerate(ids):
            o_ref[h, c * CHUNK:(c + 1) * CHUNK, :] = o_intra[i] + o_state[h]
        state_t = [s * jnp.exp(g_last[i]) + upd[i] for i, s in zip(ids, state_t)]
    for h in heads:
        s_ref[h] = state_t[h]


def _sub_block_column_selector():
    j = np.arange(SUB * HEAD_DIM) // HEAD_DIM
    c = np.arange(CHUNK) % SUB
    return jnp.asarray(j[:, None] == c[None, :], dtype=BF16)


def _out_mlp_tile(o_ref, sg_ref, h_ref, mw_ref, wout_ref, nmlp_ref, wup_ref, wdown_ref,
                  nfin_ref, out_ref, *, per_head_norm, final_norm):
    if per_head_norm:
        y = jnp.concatenate([_rms(o_ref[h], mw_ref[...]) for h in range(HEADS)], axis=1)
    else:
        y = _rms(jnp.concatenate([o_ref[h] for h in range(HEADS)], axis=1), mw_ref[...])
    y = y * sg_ref[...]
    h1 = h_ref[...] + jnp.dot(y.astype(BF16), wout_ref[...], preferred_element_type=F32)
    xn = _rms(h1, nmlp_ref[...]).astype(BF16)
    acc = h1
    yield
    for c in range(MLP_HIDDEN // D_MODEL):
        cs = slice(c * D_MODEL, (c + 1) * D_MODEL)
        up = jnp.maximum(jnp.dot(xn, wup_ref[:, cs], preferred_element_type=F32), 0.0)
        yield
        acc = acc + jnp.dot((up * up).astype(BF16), wdown_ref[cs, :], preferred_element_type=F32)
        yield
    if final_norm:
        acc = _rms(acc, nfin_ref[...])
    out_ref[...] = acc


GDN_PHASE_ORDER = "rmrmrmrmrmrmrmrmrmrmr"
HGRN_PHASE_ORDER = "rmmrmmrmmrmmrmrm"


def _rec_out_kernel(*refs, gdn, n_chunks, tiles_per_seq, n_tiles, final_norm):
    rec_refs = refs[:5]
    (sg_ref, h_ref, mw_ref, wout_ref, nmlp_ref, wup_ref, wdown_ref, nfin_ref,
     out_ref, s_ref, o_buf) = refs[5:]
    step = pl.program_id(0)

    @pl.when(step == 0)
    def _():
        o_buf[...] = jnp.zeros_like(o_buf)

    slot = step % 2
    tile = jnp.minimum(step, n_tiles - 1)
    rec = _gdn_rec_tile if gdn else _hgrn_rec_tile
    stages = {
        "r": rec(*rec_refs, o_buf.at[slot], s_ref, n_chunks=n_chunks,
                 first_tile=tile % tiles_per_seq == 0),
        "m": _out_mlp_tile(o_buf.at[1 - slot], sg_ref, h_ref, mw_ref, wout_ref, nmlp_ref,
                           wup_ref, wdown_ref, nfin_ref, out_ref, per_head_norm=gdn,
                           final_norm=final_norm),
    }
    done = set()
    for tag in GDN_PHASE_ORDER if gdn else HGRN_PHASE_ORDER:
        if next(stages[tag], "end") == "end":
            done.add(tag)
    assert done == {"r", "m"}, "phase order does not exhaust both stages"


def _rec_out(rec_inputs, gdn, sg, h, mix_norm_w, w_out, norm_mlp, w_up, w_down, norm_final,
             seq_len, final_norm):
    n = h.shape[0]
    rows = ROWS_REC
    n_tiles = n // rows
    cur_spec = pl.BlockSpec((HEADS, rows, HEAD_DIM),
                            lambda s: (0, jnp.minimum(s, n_tiles - 1), 0))
    prev_spec = pl.BlockSpec((rows, D_MODEL), lambda s: (jnp.maximum(s - 1, 0), 0))
    mw = mix_norm_w.reshape(1, -1).astype(F32)
    scratch = [pltpu.VMEM((HEADS, HEAD_DIM, HEAD_DIM), F32),
               pltpu.VMEM((2, HEADS, rows, HEAD_DIM), F32)]
    if gdn:
        rec_specs = [cur_spec] * 5
    else:
        rec_inputs = tuple(rec_inputs) + (_sub_block_column_selector(),)
        rec_specs = [cur_spec] * 4 + [_resident((SUB * HEAD_DIM, CHUNK))]
    return pl.pallas_call(
        functools.partial(_rec_out_kernel, gdn=gdn, n_chunks=rows // CHUNK,
                          tiles_per_seq=seq_len // rows, n_tiles=n_tiles, final_norm=final_norm),
        grid=(n_tiles + 1,),
        in_specs=rec_specs + [prev_spec, prev_spec,
                              _resident(mw.shape), _resident((D_MODEL, D_MODEL)),
                              _resident((1, D_MODEL)), _resident((D_MODEL, MLP_HIDDEN)),
                              _resident((MLP_HIDDEN, D_MODEL)), _resident((1, D_MODEL))],
        out_specs=prev_spec,
        out_shape=jax.ShapeDtypeStruct((n, D_MODEL), F32),
        scratch_shapes=scratch,
        compiler_params=_params(1),
        name="gdn_rec_out" if gdn else "hgrn_rec_out",
    )(*rec_inputs, sg, h, mw, w_out.astype(BF16), norm_mlp.reshape(1, D_MODEL),
      w_up.astype(BF16), w_down.astype(BF16), norm_final.reshape(1, D_MODEL))


def kernel(x, gdn_w_in, gdn_conv, gdn_a_log, gdn_dt_bias, gdn_onorm, gdn_w_out, hgrn_w_in, hgrn_lb_logits, hgrn_gnorm, hgrn_w_out, norm_mix, norm_mlp, mlp_w_up, mlp_w_down, norm_final):
    batch, seq_len, d_model = x.shape
    depth = norm_mix.shape[0]
    assert d_model == D_MODEL and seq_len % max(ROWS_IN, ROWS_REC) == 0
    h = x.reshape(batch * seq_len, d_model)
    for i in range(depth):
        j = i // 2
        gdn = i % 2 == 0
        if gdn:
            q, k, v, sg, g, beta = _gdn_in(h, norm_mix[i], gdn_w_in[j], gdn_conv[j],
                                           gdn_a_log[j], gdn_dt_bias[j], seq_len)
            rec_inputs = (q, k, v, g, beta)
            mix_norm_w, w_out = gdn_onorm[j], gdn_w_out[j]
        else:
            q, k, v, g, sg = _hgrn_in(h, norm_mix[i], hgrn_w_in[j], hgrn_lb_logits, i)
            rec_inputs = (q, k, v, g)
            mix_norm_w, w_out = hgrn_gnorm[j], hgrn_w_out[j]
        h = _rec_out(rec_inputs, gdn, sg, h, mix_norm_w, w_out, norm_mlp[i], mlp_w_up[i],
                     mlp_w_down[i], norm_final, seq_len, final_norm=(i == depth - 1))
    return h.reshape(batch, seq_len, d_model)
```

```python
import functools

import numpy as np
import jax
import jax.numpy as jnp
from jax import lax
from jax.experimental import pallas as pl
from jax.experimental.pallas import tpu as pltpu

F32 = jnp.float32
BF16 = jnp.bfloat16

D_MODEL = 1024
HEADS = 8
HEAD_DIM = 128
CHUNK = 64
SUB = 16
N_SUB = CHUNK // SUB
INV_BLOCK = 8
CONV_K = 4
MLP_HIDDEN = 4 * D_MODEL
EPS = 1e-6
LOG2_E = 1.4426950408889634
LANES = 128
HALO_ROWS = 8
COL_GROUP = 256
HEADS_PER_GROUP = COL_GROUP // HEAD_DIM
GDN_PROJ_WIDTH = 4 * D_MODEL + LANES
V7X_VMEM_LIMIT_BYTES = 56 * 1024 * 1024

ROWS_IN = 512
ROWS_REC = 256


def _resident(shape):
    nd = len(shape)
    return pl.BlockSpec(shape, lambda *_: (0,) * nd, pipeline_mode=pl.Buffered(1))


def _params(n_axes):
    return pltpu.CompilerParams(
        dimension_semantics=("arbitrary",) * n_axes,
        vmem_limit_bytes=V7X_VMEM_LIMIT_BYTES)


def _dot(a, b):
    return jnp.dot(a.astype(BF16), b.astype(BF16), preferred_element_type=F32)


def _dot_nt(a, b):
    return lax.dot_general(a.astype(BF16), b.astype(BF16),
                           (((1,), (1,)), ((), ())), preferred_element_type=F32)


def _dot_tn(a, b):
    return lax.dot_general(a.astype(BF16), b.astype(BF16),
                           (((0,), (0,)), ((), ())), preferred_element_type=F32)


def _split2(x):
    hi = x.astype(BF16)
    lo = (x - hi.astype(F32)).astype(BF16)
    return hi, lo


def _dot_split(a_split, b_split):
    a_hi, a_lo = a_split
    b_hi, b_lo = b_split
    n = b_hi.shape[1]
    rhs = jnp.concatenate([jnp.concatenate([b_hi, b_lo], axis=1),
                           jnp.concatenate([b_hi, jnp.zeros_like(b_lo)], axis=1)], axis=0)
    out = jnp.dot(jnp.concatenate([a_hi, a_lo], axis=1), rhs, preferred_element_type=F32)
    return out[:, :n] + out[:, n:]


def _tril3():
    row = lax.broadcasted_iota(jnp.int32, (CHUNK, 3 * CHUNK), 0)
    col = lax.broadcasted_iota(jnp.int32, (CHUNK, 3 * CHUNK), 1)
    return jnp.where(row >= (col & (CHUNK - 1)), 1.0, 0.0).astype(BF16)


def _cumsum_heads(tril3, tiles):
    x = jnp.concatenate(tiles, axis=1)
    x1 = x.astype(BF16)
    r1 = x - x1.astype(F32)
    x2 = r1.astype(BF16)
    x3 = (r1 - x2.astype(F32)).astype(BF16)
    out = jnp.dot(tril3, jnp.concatenate([x1, x2, x3], axis=0), preferred_element_type=F32)
    return [out[:, h * LANES:(h + 1) * LANES] for h in range(len(tiles))]


def _rms(x, w):
    ms = jnp.mean(x * x, axis=-1, keepdims=True)
    return x * lax.rsqrt(ms + EPS) * w


def _sigmoid(x):
    return 1.0 / (1.0 + jnp.exp(-x))


def _silu(x, scale=1.0):
    half = x * 0.5
    hs = half if scale == 1.0 else x * (0.5 * scale)
    return hs + hs * jnp.tanh(half)


def _softplus(x):
    return jnp.maximum(x, 0.0) + jnp.log(1.0 + jnp.exp(-jnp.abs(x)))


def _gdn_in_kernel(h_ref, nw_ref, w_ref, cw_ref, alog_ref, dtb_ref,
                   q_ref, k_ref, v_ref, sg_ref, g_ref, b_ref,
                   pbuf, halo, *, rows, tiles_per_seq):
    @pl.when(pl.program_id(0) % tiles_per_seq == 0)
    def _():
        halo[...] = jnp.zeros_like(halo)

    xn = _rms(h_ref[...], nw_ref[...]).astype(BF16)

    for ci, out_ref in enumerate((q_ref, k_ref, v_ref)):
        for grp in range(D_MODEL // COL_GROUP):
            cs = slice(ci * D_MODEL + grp * COL_GROUP, ci * D_MODEL + (grp + 1) * COL_GROUP)
            p = jnp.dot(xn, w_ref[:, cs], preferred_element_type=F32)
            for hh in range(HEADS_PER_GROUP):
                head = grp * HEADS_PER_GROUP + hh
                slab = ci * HEADS + head
                hs = slice(cs.start + hh * HEAD_DIM, cs.start + (hh + 1) * HEAD_DIM)
                ph = p[:, hh * HEAD_DIM:(hh + 1) * HEAD_DIM]
                pbuf[slab, 0:HALO_ROWS, :] = halo[slab]
                pbuf[slab, HALO_ROWS:HALO_ROWS + rows, :] = ph
                halo[slab] = ph[rows - HALO_ROWS:rows, :]
                acc = cw_ref[CONV_K - 1:CONV_K, hs] * ph
                for tap in range(CONV_K - 1):
                    start = HALO_ROWS - (CONV_K - 1) + tap
                    acc = acc + cw_ref[tap:tap + 1, hs] * pbuf[slab, start:start + rows, :]
                sl = _silu(acc)
                if ci < 2:
                    ss = jnp.sum(sl * sl, axis=-1, keepdims=True)
                    sl = sl * (lax.rsqrt(ss + EPS) * (HEAD_DIM ** -0.5 if ci == 0 else 1.0))
                out_ref[head] = sl

    for grp in range(D_MODEL // COL_GROUP):
        cs = slice(grp * COL_GROUP, (grp + 1) * COL_GROUP)
        pg = jnp.dot(xn, w_ref[:, 3 * D_MODEL + cs.start:3 * D_MODEL + cs.stop],
                     preferred_element_type=F32)
        sg_ref[:, cs] = _silu(pg)

    pab = jnp.dot(xn, w_ref[:, 4 * D_MODEL:GDN_PROJ_WIDTH], preferred_element_type=F32)
    gv = -jnp.exp(alog_ref[...]) * _softplus(pab + dtb_ref[...])
    bv = _sigmoid(pab)
    for h in range(HEADS):
        g_ref[h] = jnp.broadcast_to(gv[:, h:h + 1], (rows, LANES))
        b_ref[h] = jnp.broadcast_to(bv[:, HEADS + h:HEADS + h + 1], (rows, LANES))


def _gdn_in(h, norm_w, w_in, conv_w, a_log, dt_bias, seq_len):
    n = h.shape[0]
    rows = ROWS_IN
    w = jnp.pad(w_in, ((0, 0), (0, GDN_PROJ_WIDTH - w_in.shape[1]))).astype(BF16)
    alog = jnp.pad(a_log.astype(F32), (0, LANES - HEADS)).reshape(1, LANES)
    dtb = jnp.pad(dt_bias.astype(F32), (0, LANES - HEADS)).reshape(1, LANES)
    head_major = jax.ShapeDtypeStruct((HEADS, n, HEAD_DIM), F32)
    hm_spec = pl.BlockSpec((HEADS, rows, HEAD_DIM), lambda i: (0, i, 0))
    row_spec = pl.BlockSpec((rows, D_MODEL), lambda i: (i, 0))
    return pl.pallas_call(
        functools.partial(_gdn_in_kernel, rows=rows, tiles_per_seq=seq_len // rows),
        grid=(n // rows,),
        in_specs=[row_spec, _resident((1, D_MODEL)), _resident((D_MODEL, GDN_PROJ_WIDTH)),
                  _resident((CONV_K, 3 * D_MODEL)), _resident((1, LANES)), _resident((1, LANES))],
        out_specs=[hm_spec, hm_spec, hm_spec, row_spec, hm_spec, hm_spec],
        out_shape=[head_major, head_major, head_major,
                   jax.ShapeDtypeStruct((n, D_MODEL), F32), head_major, head_major],
        scratch_shapes=[pltpu.VMEM((3 * HEADS, HALO_ROWS + rows, HEAD_DIM), F32),
                        pltpu.VMEM((3 * HEADS, HALO_ROWS, HEAD_DIM), F32)],
        compiler_params=_params(1),
        name="gdn_in",
    )(h, norm_w.reshape(1, D_MODEL), w, conv_w, alog, dtb)


def _gdn_rec_tile(q_ref, k_ref, v_ref, g_ref, b_ref, o_ref, s_ref, *, n_chunks, first_tile):
    @pl.when(first_tile)
    def _():
        s_ref[...] = jnp.zeros_like(s_ref)

    pair = 2 * CHUNK
    n_pairs = HEADS // 2
    tril3 = _tril3()
    row = lax.broadcasted_iota(jnp.int32, (pair, pair), 0)
    col = lax.broadcasted_iota(jnp.int32, (pair, pair), 1)
    first_head = row < CHUNK
    same_head = first_head == (col < CHUNK)
    causal = same_head & (row >= col)
    strict = same_head & (row > col)
    eye = jnp.where(row == col, 1.0, 0.0).astype(F32)

    def halves(x):
        return x[:CHUNK], x[CHUNK:]

    items = [(c, p) for c in range(n_chunks) for p in range(n_pairs)]

    def load_pairs(ref):
        return [jnp.concatenate([ref[2 * p, c * CHUNK:(c + 1) * CHUNK, :],
                                 ref[2 * p + 1, c * CHUNK:(c + 1) * CHUNK, :]], axis=0)
                for c, p in items]

    q, k, v, beta = (load_pairs(r) for r in (q_ref, k_ref, v_ref, b_ref))
    gc_h = _cumsum_heads(tril3,[g_ref[h, c * CHUNK:(c + 1) * CHUNK, :]
                                       for c in range(n_chunks) for h in range(HEADS)])
    gc = [jnp.concatenate([gc_h[c * HEADS + 2 * p], gc_h[c * HEADS + 2 * p + 1]], axis=0)
          for c, p in items]
    decay = [jnp.exp(jnp.where(causal, x - x.T, -jnp.inf)) for x in gc]
    kb = [a * b for a, b in zip(k, beta)]
    kk_qk = [_dot_nt(b, jnp.concatenate([a, c_], axis=0)) for a, b, c_ in zip(kb, k, q)]
    lmat = [jnp.where(strict, x[:, :pair].T * d, 0.0) for x, d in zip(kk_qk, decay)]
    a_qk = [jnp.where(causal, x[:, pair:].T * d, 0.0) for x, d in zip(kk_qk, decay)]
    egc = [jnp.exp(x) for x in gc]
    rhs = [jnp.concatenate([a * b, kb_ * e], axis=1) for a, b, kb_, e in zip(v, beta, kb, egc)]
    def same_block(n):
        return (row // n) == (col // n)

    l_diag = [jnp.where(same_block(INV_BLOCK), x, 0.0) for x in lmat]
    pw_s = [_split2(-x) for x in l_diag]
    inv = [eye - x for x in l_diag]
    yield
    for _ in range(INV_BLOCK.bit_length() - 2):
        pw = [_dot_split(s, s) for s in pw_s]
        pw_s = [_split2(x) for x in pw]
        inv = [i + _dot_split(_split2(i), s) for i, s in zip(inv, pw_s)]
        yield
    n = INV_BLOCK
    while n < CHUNK:
        couple = same_block(2 * n) & jnp.logical_not(same_block(n))
        inv_s = [_split2(i) for i in inv]
        c_d = [_dot_split(_split2(jnp.where(couple, x, 0.0)), s) for x, s in zip(lmat, inv_s)]
        inv = [i - _dot_split(s, _split2(t)) for i, s, t in zip(inv, inv_s, c_d)]
        n *= 2
        yield
    sol = [_dot_split(_split2(i), _split2(r)) for i, r in zip(inv, rhs)]
    qe = [a * e for a, e in zip(q, egc)]
    g_last = [jnp.where(first_head, x[CHUNK - 1:CHUNK, :], x[pair - 1:pair, :]) for x in gc]
    k_dec = [a * jnp.exp(gl - x) for a, gl, x in zip(k, g_last, gc)]
    yield

    state = [s_ref[h] for h in range(HEADS)]
    for c in range(n_chunks):
        ps = range(c * n_pairs, (c + 1) * n_pairs)
        w_h = [t for i in ps for t in halves(sol[i][:, HEAD_DIM:])]
        qe_h = [t for i in ps for t in halves(qe[i])]
        wq_s = [_dot(jnp.concatenate([a, b], axis=0), s) for a, b, s in zip(w_h, qe_h, state)]
        w_s = [x[:CHUNK] for x in wq_s]
        q_s = [x[CHUNK:] for x in wq_s]
        v_new = [sol[i][:, :HEAD_DIM] - jnp.concatenate([w_s[2 * p], w_s[2 * p + 1]], axis=0)
                 for p, i in enumerate(ps)]
        o = [jnp.concatenate([q_s[2 * p], q_s[2 * p + 1]], axis=0) + _dot(a_qk[i], v_new[p])
             for p, i in enumerate(ps)]
        for p in range(n_pairs):
            o_ref[2 * p, c * CHUNK:(c + 1) * CHUNK, :] = o[p][:CHUNK]
            o_ref[2 * p + 1, c * CHUNK:(c + 1) * CHUNK, :] = o[p][CHUNK:]
        upd = [_dot_tn(a, b) for a, b in zip([t for i in ps for t in halves(k_dec[i])],
                                             [t for x in v_new for t in halves(x)])]
        state = [s * jnp.exp(gc_h[c * HEADS + h][CHUNK - 1:CHUNK, :]) + d
                 for h, (s, d) in enumerate(zip(state, upd))]
        if c + 1 < n_chunks:
            yield
    for h in range(HEADS):
        s_ref[h] = state[h]


def _hgrn_in_kernel(h_ref, nw_ref, w_ref, lg_ref,
                    q_ref, k_ref, v_ref, g_ref, sg_ref, *, layer):
    lg = lg_ref[...]
    m = jnp.max(lg, axis=0, keepdims=True)
    e = jnp.exp(lg - m)
    sm = e / jnp.sum(e, axis=0, keepdims=True)
    cs = sm[0:1, :]
    for r in range(1, layer + 1):
        cs = cs + sm[r:r + 1, :]
    lb = cs - sm[0:1, :]
    log_lb = jnp.log(lb)
    log_1m_lb = jnp.log1p(-lb)

    xn = _rms(h_ref[...], nw_ref[...]).astype(BF16)

    def proj(i, cs):
        return jnp.dot(xn, w_ref[:, i * D_MODEL + cs.start:i * D_MODEL + cs.stop],
                       preferred_element_type=F32)

    def heads_out(ref, grp, val):
        for hh in range(HEADS_PER_GROUP):
            ref[grp * HEADS_PER_GROUP + hh] = val[:, hh * HEAD_DIM:(hh + 1) * HEAD_DIM]

    def finish(kind, grp, p):
        cs = slice(grp * COL_GROUP, (grp + 1) * COL_GROUP)
        if kind == 0:
            heads_out(q_ref, grp, _silu(p, HEAD_DIM ** -0.5))
        elif kind == 1:
            t = jnp.exp(-jnp.abs(p))
            log_sig = jnp.minimum(p, 0.0) - jnp.log(1.0 + t)
            a = log_lb[:, cs]
            b = log_1m_lb[:, cs] + log_sig
            heads_out(g_ref, grp, jnp.maximum(a, b) + jnp.log(1.0 + jnp.exp(-jnp.abs(a - b))))
            inv = 1.0 / (1.0 + t)
            heads_out(k_ref, grp, (1.0 - lb[:, cs]) * jnp.where(p >= 0.0, t * inv, inv))
        elif kind == 2:
            heads_out(v_ref, grp, p)
        else:
            sg_ref[:, cs] = _silu(p)

    for grp in range(D_MODEL // COL_GROUP):
        for kind in (1, 0, 3, 2):
            finish(kind, grp, proj(kind, slice(grp * COL_GROUP, (grp + 1) * COL_GROUP)))


def _hgrn_in(h, norm_w, w_in, lb_logits, layer):
    n = h.shape[0]
    rows = ROWS_IN
    depth = lb_logits.shape[0]
    head_major = jax.ShapeDtypeStruct((HEADS, n, HEAD_DIM), F32)
    hm_spec = pl.BlockSpec((HEADS, rows, HEAD_DIM), lambda i: (0, i, 0))
    row_spec = pl.BlockSpec((rows, D_MODEL), lambda i: (i, 0))
    return pl.pallas_call(
        functools.partial(_hgrn_in_kernel, layer=layer),
        grid=(n // rows,),
        in_specs=[row_spec, _resident((1, D_MODEL)), _resident((D_MODEL, 4 * D_MODEL)),
                  _resident((depth, D_MODEL))],
        out_specs=[hm_spec, hm_spec, hm_spec, hm_spec, row_spec],
        out_shape=[head_major] * 4 + [jax.ShapeDtypeStruct((n, D_MODEL), F32)],
        compiler_params=_params(1),
        name="hgrn_in",
    )(h, norm_w.reshape(1, D_MODEL), w_in.astype(BF16), lb_logits.astype(F32))


def _hgrn_rec_tile(q_ref, k_ref, v_ref, g_ref, ecol_ref, o_ref, s_ref, *, n_chunks, first_tile):
    @pl.when(first_tile)
    def _():
        s_ref[...] = jnp.zeros_like(s_ref)

    tril3 = _tril3()
    row = lax.broadcasted_iota(jnp.int32, (CHUNK, CHUNK), 0)
    col = lax.broadcasted_iota(jnp.int32, (CHUNK, CHUNK), 1)
    same_sub = ((row // SUB) == (col // SUB)) & (row >= col)
    sub_i = lax.broadcasted_iota(jnp.int32, (N_SUB, SUB, HEAD_DIM), 1)
    heads = range(HEADS)

    def diag_products(q, k, gc):
        g3 = (gc * LOG2_E).reshape(N_SUB, SUB, HEAD_DIM)
        q3 = q.reshape(N_SUB, SUB, HEAD_DIM)
        k3 = k.reshape(N_SUB, SUB, HEAD_DIM)
        half = SUB // 2
        zeros = jnp.zeros((N_SUB, half, HEAD_DIM), F32)
        pieces = []
        for j in range(SUB):
            lo = 0 if j < half else half
            dec = jnp.exp2(jnp.minimum(g3[:, lo:, :] - g3[:, j:j + 1, :], 0.0))
            piece = q3[:, lo:, :] * dec * k3[:, j:j + 1, :]
            if lo:
                piece = jnp.concatenate([zeros, piece], axis=1)
            pieces.append(piece.reshape(CHUNK, HEAD_DIM).astype(BF16))
        return jnp.concatenate(pieces, axis=1)

    items = [(c, h) for c in range(n_chunks) for h in heads]

    def tiles(ref):
        return [ref[h, c * CHUNK:(c + 1) * CHUNK, :] for c, h in items]

    q, k, v, g = (tiles(r) for r in (q_ref, k_ref, v_ref, g_ref))
    gc = _cumsum_heads(tril3, g)

    gb = [x - y for x, y in zip(gc, g)]
    a_rows = [[jnp.zeros((SUB, CHUNK), F32)] for _ in items]
    for s in range(1, N_SUB):
        lo, hi = s * SUB, (s + 1) * SUB
        for i in range(len(items)):
            gb_s = gb[i][lo:lo + 1, :]
            q_off = q[i][lo:hi, :] * jnp.exp(gc[i][lo:hi, :] - gb_s)
            k_off = jnp.concatenate(
                [k[i][:lo, :] * jnp.exp(gb_s - gc[i][:lo, :]),
                 jnp.zeros((CHUNK - lo, HEAD_DIM), F32)], axis=0)
            a_rows[i].append(_dot_nt(q_off, k_off))
    qe = [a * jnp.exp(x) for a, x in zip(q, gc)]
    g_last = [x[CHUNK - 1:CHUNK, :] for x in gc]
    upd = [_dot_tn(a, b * jnp.exp(gl - x)) for a, b, gl, x in zip(v, k, g_last, gc)]
    yield

    o_intra = []
    for c in range(n_chunks):
        ids = range(c * HEADS, (c + 1) * HEADS)
        prods = jnp.concatenate([diag_products(q[i], k[i], gc[i]) for i in ids], axis=0)
        a_all = jnp.dot(prods, ecol_ref[...], preferred_element_type=F32)
        a_mat = [jnp.where(same_sub, a_all[h * CHUNK:(h + 1) * CHUNK], 0.0)
                 + jnp.concatenate(a_rows[i], axis=0) for h, i in enumerate(ids)]
        o_intra += [_dot(a, v[i]) for a, i in zip(a_mat, ids)]
        yield

    state_t = [s_ref[h] for h in heads]
    for c in range(n_chunks):
        ids = range(c * HEADS, (c + 1) * HEADS)
        o_state = [_dot_nt(qe[i], s) for i, s in zip(ids, state_t)]
        for h, i in enumerate(ids):
            o_ref[h, c * CHUNK:(c + 1) * CHUNK, :] = o_intra[i] + o_state[h]
        state_t = [s * jnp.exp(g_last[i]) + upd[i] for i, s in zip(ids, state_t)]
    for h in heads:
        s_ref[h] = state_t[h]


def _sub_block_column_selector():
    j = np.arange(SUB * HEAD_DIM) // HEAD_DIM
    c = np.arange(CHUNK) % SUB
    return jnp.asarray(j[:, None] == c[None, :], dtype=BF16)


def _out_mlp_tile(o_ref, sg_ref, h_ref, mw_ref, wout_ref, nmlp_ref, wup_ref, wdown_ref,
                  nfin_ref, out_ref, *, per_head_norm, final_norm):
    if per_head_norm:
        y = jnp.concatenate([_rms(o_ref[h], mw_ref[...]) for h in range(HEADS)], axis=1)
    else:
        y = _rms(jnp.concatenate([o_ref[h] for h in range(HEADS)], axis=1), mw_ref[...])
    y = y * sg_ref[...]
    h1 = h_ref[...] + jnp.dot(y.astype(BF16), wout_ref[...], preferred_element_type=F32)
    xn = _rms(h1, nmlp_ref[...]).astype(BF16)
    acc = h1
    yield
    for c in range(MLP_HIDDEN // D_MODEL):
        cs = slice(c * D_MODEL, (c + 1) * D_MODEL)
        up = jnp.maximum(jnp.dot(xn, wup_ref[:, cs], preferred_element_type=F32), 0.0)
        yield
        acc = acc + jnp.dot((up * up).astype(BF16), wdown_ref[cs, :], preferred_element_type=F32)
        yield
    if final_norm:
        acc = _rms(acc, nfin_ref[...])
    out_ref[...] = acc


GDN_PHASE_ORDER = "rmrmrmrmrmrmrmrmrmrmr"
HGRN_PHASE_ORDER = "rmmrmmrmmrmmrmrm"


def _rec_out_kernel(*refs, gdn, n_chunks, tiles_per_seq, n_tiles, final_norm):
    rec_refs = refs[:5]
    (sg_ref, h_ref, mw_ref, wout_ref, nmlp_ref, wup_ref, wdown_ref, nfin_ref,
     out_ref, s_ref, o_buf) = refs[5:]
    step = pl.program_id(0)

    @pl.when(step == 0)
    def _():
        o_buf[...] = jnp.zeros_like(o_buf)

    slot = step % 2
    tile = jnp.minimum(step, n_tiles - 1)
    rec = _gdn_rec_tile if gdn else _hgrn_rec_tile
    stages = {
        "r": rec(*rec_refs, o_buf.at[slot], s_ref, n_chunks=n_chunks,
                 first_tile=tile % tiles_per_seq == 0),
        "m": _out_mlp_tile(o_buf.at[1 - slot], sg_ref, h_ref, mw_ref, wout_ref, nmlp_ref,
                           wup_ref, wdown_ref, nfin_ref, out_ref, per_head_norm=gdn,
                           final_norm=final_norm),
    }
    done = set()
    for tag in GDN_PHASE_ORDER if gdn else HGRN_PHASE_ORDER:
        if next(stages[tag], "end") == "end":
            done.add(tag)
    assert done == {"r", "m"}, "phase order does not exhaust both stages"


def _rec_out(rec_inputs, gdn, sg, h, mix_norm_w, w_out, norm_mlp, w_up, w_down, norm_final,
             seq_len, final_norm):
    n = h.shape[0]
    rows = ROWS_REC
    n_tiles = n // rows
    cur_spec = pl.BlockSpec((HEADS, rows, HEAD_DIM),
                            lambda s: (0, jnp.minimum(s, n_tiles - 1), 0))
    prev_spec = pl.BlockSpec((rows, D_MODEL), lambda s: (jnp.maximum(s - 1, 0), 0))
    mw = mix_norm_w.reshape(1, -1).astype(F32)
    scratch = [pltpu.VMEM((HEADS, HEAD_DIM, HEAD_DIM), F32),
               pltpu.VMEM((2, HEADS, rows, HEAD_DIM), F32)]
    if gdn:
        rec_specs = [cur_spec] * 5
    else:
        rec_inputs = tuple(rec_inputs) + (_sub_block_column_selector(),)
        rec_specs = [cur_spec] * 4 + [_resident((SUB * HEAD_DIM, CHUNK))]
    return pl.pallas_call(
        functools.partial(_rec_out_kernel, gdn=gdn, n_chunks=rows // CHUNK,
                          tiles_per_seq=seq_len // rows, n_tiles=n_tiles, final_norm=final_norm),
        grid=(n_tiles + 1,),
        in_specs=rec_specs + [prev_spec, prev_spec,
                              _resident(mw.shape), _resident((D_MODEL, D_MODEL)),
                              _resident((1, D_MODEL)), _resident((D_MODEL, MLP_HIDDEN)),
                              _resident((MLP_HIDDEN, D_MODEL)), _resident((1, D_MODEL))],
        out_specs=prev_spec,
        out_shape=jax.ShapeDtypeStruct((n, D_MODEL), F32),
        scratch_shapes=scratch,
        compiler_params=_params(1),
        name="gdn_rec_out" if gdn else "hgrn_rec_out",
    )(*rec_inputs, sg, h, mw, w_out.astype(BF16), norm_mlp.reshape(1, D_MODEL),
      w_up.astype(BF16), w_down.astype(BF16), norm_final.reshape(1, D_MODEL))


def kernel(x, gdn_w_in, gdn_conv, gdn_a_log, gdn_dt_bias, gdn_onorm, gdn_w_out, hgrn_w_in, hgrn_lb_logits, hgrn_gnorm, hgrn_w_out, norm_mix, norm_mlp, mlp_w_up, mlp_w_down, norm_final):
    batch, seq_len, d_model = x.shape
    depth = norm_mix.shape[0]
    assert d_model == D_MODEL and seq_len % max(ROWS_IN, ROWS_REC) == 0
    h = x.reshape(batch * seq_len, d_model)
    for i in range(depth):
        j = i // 2
        gdn = i % 2 == 0
        if gdn:
            q, k, v, sg, g, beta = _gdn_in(h, norm_mix[i], gdn_w_in[j], gdn_conv[j],
                                           gdn_a_log[j], gdn_dt_bias[j], seq_len)
            rec_inputs = (q, k, v, g, beta)
            mix_norm_w, w_out = gdn_onorm[j], gdn_w_out[j]
        else:
            q, k, v, g, sg = _hgrn_in(h, norm_mix[i], hgrn_w_in[j], hgrn_lb_logits, i)
            rec_inputs = (q, k, v, g)
            mix_norm_w, w_out = hgrn_gnorm[j], hgrn_w_out[j]
        h = _rec_out(rec_inputs, gdn, sg, h, mix_norm_w, w_out, norm_mlp[i], mlp_w_up[i],
                     mlp_w_down[i], norm_final, seq_len, final_norm=(i == depth - 1))
    return h.reshape(batch, seq_len, d_model)
```

```python
import functools

import numpy as np
import jax
import jax.numpy as jnp
from jax import lax
from jax.experimental import pallas as pl
from jax.experimental.pallas import tpu as pltpu

F32 = jnp.float32
BF16 = jnp.bfloat16

D_MODEL = 1024
HEADS = 8
HEAD_DIM = 128
CHUNK = 64
SUB = 16
N_SUB = CHUNK // SUB
INV_BLOCK = 8
CONV_K = 4
MLP_HIDDEN = 4 * D_MODEL
EPS = 1e-6
LOG2_E = 1.4426950408889634
LANES = 128
HALO_ROWS = 8
COL_GROUP = 256
HEADS_PER_GROUP = COL_GROUP // HEAD_DIM
GDN_PROJ_WIDTH = 4 * D_MODEL + LANES
V7X_VMEM_LIMIT_BYTES = 56 * 1024 * 1024

ROWS_IN = 512
ROWS_REC = 256


def _resident(shape):
    nd = len(shape)
    return pl.BlockSpec(shape, lambda *_: (0,) * nd, pipeline_mode=pl.Buffered(1))


def _params(n_axes):
    return pltpu.CompilerParams(
        dimension_semantics=("arbitrary",) * n_axes,
        vmem_limit_bytes=V7X_VMEM_LIMIT_BYTES)


def _dot(a, b):
    return jnp.dot(a.astype(BF16), b.astype(BF16), preferred_element_type=F32)


def _dot_nt(a, b):
    return lax.dot_general(a.astype(BF16), b.astype(BF16),
                           (((1,), (1,)), ((), ())), preferred_element_type=F32)


def _dot_tn(a, b):
    return lax.dot_general(a.astype(BF16), b.astype(BF16),
                           (((0,), (0,)), ((), ())), preferred_element_type=F32)


def _split2(x):
    hi = x.astype(BF16)
    lo = (x - hi.astype(F32)).astype(BF16)
    return hi, lo


def _dot_split(a_split, b_split):
    a_hi, a_lo = a_split
    b_hi, b_lo = b_split
    n = b_hi.shape[1]
    rhs = jnp.concatenate([jnp.concatenate([b_hi, b_lo], axis=1),
                           jnp.concatenate([b_hi, jnp.zeros_like(b_lo)], axis=1)], axis=0)
    out = jnp.dot(jnp.concatenate([a_hi, a_lo], axis=1), rhs, preferred_element_type=F32)
    return out[:, :n] + out[:, n:]


def _tril3():
    row = lax.broadcasted_iota(jnp.int32, (CHUNK, 3 * CHUNK), 0)
    col = lax.broadcasted_iota(jnp.int32, (CHUNK, 3 * CHUNK), 1)
    return jnp.where(row >= (col & (CHUNK - 1)), 1.0, 0.0).astype(BF16)


def _cumsum_heads(tril3, tiles):
    x = jnp.concatenate(tiles, axis=1)
    x1 = x.astype(BF16)
    r1 = x - x1.astype(F32)
    x2 = r1.astype(BF16)
    x3 = (r1 - x2.astype(F32)).astype(BF16)
    out = jnp.dot(tril3, jnp.concatenate([x1, x2, x3], axis=0), preferred_element_type=F32)
    return [out[:, h * LANES:(h + 1) * LANES] for h in range(len(tiles))]


def _rms(x, w):
    ms = jnp.mean(x * x, axis=-1, keepdims=True)
    return x * lax.rsqrt(ms + EPS) * w


def _sigmoid(x):
    return 1.0 / (1.0 + jnp.exp(-x))


def _silu(x, scale=1.0):
    half = x * 0.5
    hs = half if scale == 1.0 else x * (0.5 * scale)
    return hs + hs * jnp.tanh(half)


def _softplus(x):
    return jnp.maximum(x, 0.0) + jnp.log(1.0 + jnp.exp(-jnp.abs(x)))


def _gdn_in_kernel(h_ref, nw_ref, w_ref, cw_ref, alog_ref, dtb_ref,
                   q_ref, k_ref, v_ref, sg_ref, gb_ref,
                   pbuf, halo, *, rows, tiles_per_seq):
    @pl.when(pl.program_id(0) % tiles_per_seq == 0)
    def _():
        halo[...] = jnp.zeros_like(halo)

    xn = _rms(h_ref[...], nw_ref[...]).astype(BF16)

    for ci, out_ref in enumerate((q_ref, k_ref, v_ref)):
        for grp in range(D_MODEL // COL_GROUP):
            cs = slice(ci * D_MODEL + grp * COL_GROUP, ci * D_MODEL + (grp + 1) * COL_GROUP)
            p = jnp.dot(xn, w_ref[:, cs], preferred_element_type=F32)
            for hh in range(HEADS_PER_GROUP):
                head = grp * HEADS_PER_GROUP + hh
                slab = ci * HEADS + head
                hs = slice(cs.start + hh * HEAD_DIM, cs.start + (hh + 1) * HEAD_DIM)
                ph = p[:, hh * HEAD_DIM:(hh + 1) * HEAD_DIM]
                pbuf[slab, 0:HALO_ROWS, :] = halo[slab]
                pbuf[slab, HALO_ROWS:HALO_ROWS + rows, :] = ph
                halo[slab] = ph[rows - HALO_ROWS:rows, :]
                acc = cw_ref[CONV_K - 1:CONV_K, hs] * ph
                for tap in range(CONV_K - 1):
                    start = HALO_ROWS - (CONV_K - 1) + tap
                    acc = acc + cw_ref[tap:tap + 1, hs] * pbuf[slab, start:start + rows, :]
                sl = _silu(acc)
                if ci < 2:
                    ss = jnp.sum(sl * sl, axis=-1, keepdims=True)
                    sl = sl * (lax.rsqrt(ss + EPS) * (HEAD_DIM ** -0.5 if ci == 0 else 1.0))
                out_ref[head] = sl

    for grp in range(D_MODEL // COL_GROUP):
        cs = slice(grp * COL_GROUP, (grp + 1) * COL_GROUP)
        pg = jnp.dot(xn, w_ref[:, 3 * D_MODEL + cs.start:3 * D_MODEL + cs.stop],
                     preferred_element_type=F32)
        sg_ref[:, cs] = _silu(pg)

    pab = jnp.dot(xn, w_ref[:, 4 * D_MODEL:GDN_PROJ_WIDTH], preferred_element_type=F32)
    gv = -jnp.exp(alog_ref[...]) * _softplus(pab + dtb_ref[...])
    bv = _sigmoid(pab)
    lane = lax.broadcasted_iota(jnp.int32, (rows, LANES), 1)
    gb_ref[...] = jnp.where(lane < HEADS, gv, bv)


def _gdn_in(h, norm_w, w_in, conv_w, a_log, dt_bias, seq_len):
    n = h.shape[0]
    rows = ROWS_IN
    w = jnp.pad(w_in, ((0, 0), (0, GDN_PROJ_WIDTH - w_in.shape[1]))).astype(BF16)
    alog = jnp.pad(a_log.astype(F32), (0, LANES - HEADS)).reshape(1, LANES)
    dtb = jnp.pad(dt_bias.astype(F32), (0, LANES - HEADS)).reshape(1, LANES)
    head_major = jax.ShapeDtypeStruct((HEADS, n, HEAD_DIM), F32)
    hm_spec = pl.BlockSpec((HEADS, rows, HEAD_DIM), lambda i: (0, i, 0))
    row_spec = pl.BlockSpec((rows, D_MODEL), lambda i: (i, 0))
    return pl.pallas_call(
        functools.partial(_gdn_in_kernel, rows=rows, tiles_per_seq=seq_len // rows),
        grid=(n // rows,),
        in_specs=[row_spec, _resident((1, D_MODEL)), _resident((D_MODEL, GDN_PROJ_WIDTH)),
                  _resident((CONV_K, 3 * D_MODEL)), _resident((1, LANES)), _resident((1, LANES))],
        out_specs=[hm_spec, hm_spec, hm_spec, row_spec,
                   pl.BlockSpec((rows, LANES), lambda i: (i, 0))],
        out_shape=[head_major, head_major, head_major,
                   jax.ShapeDtypeStruct((n, D_MODEL), F32),
                   jax.ShapeDtypeStruct((n, LANES), F32)],
        scratch_shapes=[pltpu.VMEM((3 * HEADS, HALO_ROWS + rows, HEAD_DIM), F32),
                        pltpu.VMEM((3 * HEADS, HALO_ROWS, HEAD_DIM), F32)],
        compiler_params=_params(1),
        name="gdn_in",
    )(h, norm_w.reshape(1, D_MODEL), w, conv_w, alog, dtb)


def _gdn_rec_tile(q_ref, k_ref, v_ref, gb_ref, o_ref, s_ref, *, n_chunks, first_tile):
    @pl.when(first_tile)
    def _():
        s_ref[...] = jnp.zeros_like(s_ref)

    pair = 2 * CHUNK
    n_pairs = HEADS // 2
    tril3 = _tril3()
    row = lax.broadcasted_iota(jnp.int32, (pair, pair), 0)
    col = lax.broadcasted_iota(jnp.int32, (pair, pair), 1)
    first_head = row < CHUNK
    same_head = first_head == (col < CHUNK)
    causal = same_head & (row >= col)
    strict = same_head & (row > col)
    eye = jnp.where(row == col, 1.0, 0.0).astype(F32)

    def halves(x):
        return x[:CHUNK], x[CHUNK:]

    items = [(c, p) for c in range(n_chunks) for p in range(n_pairs)]

    def load_pairs(ref):
        return [jnp.concatenate([ref[2 * p, c * CHUNK:(c + 1) * CHUNK, :],
                                 ref[2 * p + 1, c * CHUNK:(c + 1) * CHUNK, :]], axis=0)
                for c, p in items]

    q, k, v = (load_pairs(r) for r in (q_ref, k_ref, v_ref))
    gb = [gb_ref[c * CHUNK:(c + 1) * CHUNK, :] for c in range(n_chunks)]
    gc_c = _cumsum_heads(tril3, gb)

    def lane_bcast(x, lane):
        return jnp.broadcast_to(x[:, lane:lane + 1], (CHUNK, LANES))

    gc_h = [lane_bcast(gc_c[c], h) for c in range(n_chunks) for h in range(HEADS)]
    beta_h = [lane_bcast(gb[c], HEADS + h) for c in range(n_chunks) for h in range(HEADS)]
    gc = [jnp.concatenate([gc_h[c * HEADS + 2 * p], gc_h[c * HEADS + 2 * p + 1]], axis=0)
          for c, p in items]
    beta = [jnp.concatenate([beta_h[c * HEADS + 2 * p], beta_h[c * HEADS + 2 * p + 1]], axis=0)
            for c, p in items]
    decay = [jnp.exp(jnp.where(causal, x - x.T, -jnp.inf)) for x in gc]
    kb = [a * b for a, b in zip(k, beta)]
    kk_qk = [_dot_nt(b, jnp.concatenate([a, c_], axis=0)) for a, b, c_ in zip(kb, k, q)]
    lmat = [jnp.where(strict, x[:, :pair].T * d, 0.0) for x, d in zip(kk_qk, decay)]
    a_qk = [jnp.where(causal, x[:, pair:].T * d, 0.0) for x, d in zip(kk_qk, decay)]
    egc = [jnp.exp(x) for x in gc]
    rhs = [jnp.concatenate([a * b, kb_ * e], axis=1) for a, b, kb_, e in zip(v, beta, kb, egc)]
    def same_block(n):
        return (row // n) == (col // n)

    l_diag = [jnp.where(same_block(INV_BLOCK), x, 0.0) for x in lmat]
    pw_s = [_split2(-x) for x in l_diag]
    inv = [eye - x for x in l_diag]
    yield
    for _ in range(INV_BLOCK.bit_length() - 2):
        pw = [_dot_split(s, s) for s in pw_s]
        pw_s = [_split2(x) for x in pw]
        inv = [i + _dot_split(_split2(i), s) for i, s in zip(inv, pw_s)]
        yield
    n = INV_BLOCK
    while n < CHUNK:
        couple = same_block(2 * n) & jnp.logical_not(same_block(n))
        inv_s = [_split2(i) for i in inv]
        c_d = [_dot_split(_split2(jnp.where(couple, x, 0.0)), s) for x, s in zip(lmat, inv_s)]
        inv = [i - _dot_split(s, _split2(t)) for i, s, t in zip(inv, inv_s, c_d)]
        n *= 2
        yield
    sol = [_dot_split(_split2(i), _split2(r)) for i, r in zip(inv, rhs)]
    qe = [a * e for a, e in zip(q, egc)]
    g_last = [jnp.where(first_head, x[CHUNK - 1:CHUNK, :], x[pair - 1:pair, :]) for x in gc]
    k_dec = [a * jnp.exp(gl - x) for a, gl, x in zip(k, g_last, gc)]
    yield

    state = [s_ref[h] for h in range(HEADS)]
    for c in range(n_chunks):
        ps = range(c * n_pairs, (c + 1) * n_pairs)
        w_h = [t for i in ps for t in halves(sol[i][:, HEAD_DIM:])]
        qe_h = [t for i in ps for t in halves(qe[i])]
        wq_s = [_dot(jnp.concatenate([a, b], axis=0), s) for a, b, s in zip(w_h, qe_h, state)]
        w_s = [x[:CHUNK] for x in wq_s]
        q_s = [x[CHUNK:] for x in wq_s]
        v_new = [sol[i][:, :HEAD_DIM] - jnp.concatenate([w_s[2 * p], w_s[2 * p + 1]], axis=0)
                 for p, i in enumerate(ps)]
        o = [jnp.concatenate([q_s[2 * p], q_s[2 * p + 1]], axis=0) + _dot(a_qk[i], v_new[p])
             for p, i in enumerate(ps)]
        for p in range(n_pairs):
            o_ref[2 * p, c * CHUNK:(c + 1) * CHUNK, :] = o[p][:CHUNK]
            o_ref[2 * p + 1, c * CHUNK:(c + 1) * CHUNK, :] = o[p][CHUNK:]
        upd = [_dot_tn(a, b) for a, b in zip([t for i in ps for t in halves(k_dec[i])],
                                             [t for x in v_new for t in halves(x)])]
        state = [s * jnp.exp(gc_h[c * HEADS + h][CHUNK - 1:CHUNK, :]) + d
                 for h, (s, d) in enumerate(zip(state, upd))]
        if c + 1 < n_chunks:
            yield
    for h in range(HEADS):
        s_ref[h] = state[h]


def _hgrn_in_kernel(h_ref, nw_ref, w_ref, lg_ref,
                    q_ref, k_ref, v_ref, g_ref, sg_ref, *, layer):
    lg = lg_ref[...]
    m = jnp.max(lg, axis=0, keepdims=True)
    e = jnp.exp(lg - m)
    sm = e / jnp.sum(e, axis=0, keepdims=True)
    cs = sm[0:1, :]
    for r in range(1, layer + 1):
        cs = cs + sm[r:r + 1, :]
    lb = cs - sm[0:1, :]
    log_lb = jnp.log(lb)
    log_1m_lb = jnp.log1p(-lb)

    xn = _rms(h_ref[...], nw_ref[...]).astype(BF16)

    def proj(i, cs):
        return jnp.dot(xn, w_ref[:, i * D_MODEL + cs.start:i * D_MODEL + cs.stop],
                       preferred_element_type=F32)

    def heads_out(ref, grp, val):
        for hh in range(HEADS_PER_GROUP):
            ref[grp * HEADS_PER_GROUP + hh] = val[:, hh * HEAD_DIM:(hh + 1) * HEAD_DIM]

    def finish(kind, grp, p):
        cs = slice(grp * COL_GROUP, (grp + 1) * COL_GROUP)
        if kind == 0:
            heads_out(q_ref, grp, _silu(p, HEAD_DIM ** -0.5))
        elif kind == 1:
            t = jnp.exp(-jnp.abs(p))
            log_sig = jnp.minimum(p, 0.0) - jnp.log(1.0 + t)
            a = log_lb[:, cs]
            b = log_1m_lb[:, cs] + log_sig
            heads_out(g_ref, grp, jnp.maximum(a, b) + jnp.log(1.0 + jnp.exp(-jnp.abs(a - b))))
            inv = 1.0 / (1.0 + t)
            heads_out(k_ref, grp, (1.0 - lb[:, cs]) * jnp.where(p >= 0.0, t * inv, inv))
        elif kind == 2:
            heads_out(v_ref, grp, p)
        else:
            sg_ref[:, cs] = _silu(p)

    for grp in range(D_MODEL // COL_GROUP):
        for kind in (1, 0, 3, 2):
            finish(kind, grp, proj(kind, slice(grp * COL_GROUP, (grp + 1) * COL_GROUP)))


def _hgrn_in(h, norm_w, w_in, lb_logits, layer):
    n = h.shape[0]
    rows = ROWS_IN
    depth = lb_logits.shape[0]
    head_major = jax.ShapeDtypeStruct((HEADS, n, HEAD_DIM), F32)
    hm_spec = pl.BlockSpec((HEADS, rows, HEAD_DIM), lambda i: (0, i, 0))
    row_spec = pl.BlockSpec((rows, D_MODEL), lambda i: (i, 0))
    return pl.pallas_call(
        functools.partial(_hgrn_in_kernel, layer=layer),
        grid=(n // rows,),
        in_specs=[row_spec, _resident((1, D_MODEL)), _resident((D_MODEL, 4 * D_MODEL)),
                  _resident((depth, D_MODEL))],
        out_specs=[hm_spec, hm_spec, hm_spec, hm_spec, row_spec],
        out_shape=[head_major] * 4 + [jax.ShapeDtypeStruct((n, D_MODEL), F32)],
        compiler_params=_params(1),
        name="hgrn_in",
    )(h, norm_w.reshape(1, D_MODEL), w_in.astype(BF16), lb_logits.astype(F32))


def _hgrn_rec_tile(q_ref, k_ref, v_ref, g_ref, ecol_ref, o_ref, s_ref, *, n_chunks, first_tile):
    @pl.when(first_tile)
    def _():
        s_ref[...] = jnp.zeros_like(s_ref)

    tril3 = _tril3()
    row = lax.broadcasted_iota(jnp.int32, (CHUNK, CHUNK), 0)
    col = lax.broadcasted_iota(jnp.int32, (CHUNK, CHUNK), 1)
    same_sub = ((row // SUB) == (col // SUB)) & (row >= col)
    sub_i = lax.broadcasted_iota(jnp.int32, (N_SUB, SUB, HEAD_DIM), 1)
    heads = range(HEADS)

    def diag_products(q, k, gc):
        g3 = (gc * LOG2_E).reshape(N_SUB, SUB, HEAD_DIM)
        q3 = q.reshape(N_SUB, SUB, HEAD_DIM)
        k3 = k.reshape(N_SUB, SUB, HEAD_DIM)
        half = SUB // 2
        zeros = jnp.zeros((N_SUB, half, HEAD_DIM), F32)
        pieces = []
        for j in range(SUB):
            lo = 0 if j < half else half
            dec = jnp.exp2(jnp.minimum(g3[:, lo:, :] - g3[:, j:j + 1, :], 0.0))
            piece = q3[:, lo:, :] * dec * k3[:, j:j + 1, :]
            if lo:
                piece = jnp.concatenate([zeros, piece], axis=1)
            pieces.append(piece.reshape(CHUNK, HEAD_DIM).astype(BF16))
        return jnp.concatenate(pieces, axis=1)

    items = [(c, h) for c in range(n_chunks) for h in heads]

    def tiles(ref):
        return [ref[h, c * CHUNK:(c + 1) * CHUNK, :] for c, h in items]

    q, k, v, g = (tiles(r) for r in (q_ref, k_ref, v_ref, g_ref))
    gc = _cumsum_heads(tril3, g)

    gb = [x - y for x, y in zip(gc, g)]
    a_rows = [[jnp.zeros((SUB, CHUNK), F32)] for _ in items]
    for s in range(1, N_SUB):
        lo, hi = s * SUB, (s + 1) * SUB
        for i in range(len(items)):
            gb_s = gb[i][lo:lo + 1, :]
            q_off = q[i][lo:hi, :] * jnp.exp(gc[i][lo:hi, :] - gb_s)
            k_off = jnp.concatenate(
                [k[i][:lo, :] * jnp.exp(gb_s - gc[i][:lo, :]),
                 jnp.zeros((CHUNK - lo, HEAD_DIM), F32)], axis=0)
            a_rows[i].append(_dot_nt(q_off, k_off))
    qe = [a * jnp.exp(x) for a, x in zip(q, gc)]
    g_last = [x[CHUNK - 1:CHUNK, :] for x in gc]
    upd = [_dot_tn(a, b * jnp.exp(gl - x)) for a, b, gl, x in zip(v, k, g_last, gc)]
    yield

    o_intra = []
    for c in range(n_chunks):
        ids = range(c * HEADS, (c + 1) * HEADS)
        prods = jnp.concatenate([diag_products(q[i], k[i], gc[i]) for i in ids], axis=0)
        a_all = jnp.dot(prods, ecol_ref[...], preferred_element_type=F32)
        a_mat = [jnp.where(same_sub, a_all[h * CHUNK:(h + 1) * CHUNK], 0.0)
                 + jnp.concatenate(a_rows[i], axis=0) for h, i in enumerate(ids)]
        o_intra += [_dot(a, v[i]) for a, i in zip(a_mat, ids)]
        yield

    state_t = [s_ref[h] for h in heads]
    for c in range(n_chunks):
        ids = range(c * HEADS, (c + 1) * HEADS)
        o_state = [_dot_nt(qe[i], s) for i, s in zip(ids, state_t)]
        for h, i in enumerate(ids):
            o_ref[h, c * CHUNK:(c + 1) * CHUNK, :] = o_intra[i] + o_state[h]
        state_t = [s * jnp.exp(g_last[i]) + upd[i] for i, s in zip(ids, state_t)]
    for h in heads:
        s_ref[h] = state_t[h]


def _sub_block_column_selector():
    j = np.arange(SUB * HEAD_DIM) // HEAD_DIM
    c = np.arange(CHUNK) % SUB
    return jnp.asarray(j[:, None] == c[None, :], dtype=BF16)


def _out_mlp_tile(o_ref, sg_ref, h_ref, mw_ref, wout_ref, nmlp_ref, wup_ref, wdown_ref,
                  nfin_ref, out_ref, *, per_head_norm, final_norm):
    if per_head_norm:
        y = jnp.concatenate([_rms(o_ref[h], mw_ref[...]) for h in range(HEADS)], axis=1)
    else:
        y = _rms(jnp.concatenate([o_ref[h] for h in range(HEADS)], axis=1), mw_ref[...])
    y = y * sg_ref[...]
    h1 = h_ref[...] + jnp.dot(y.astype(BF16), wout_ref[...], preferred_element_type=F32)
    xn = _rms(h1, nmlp_ref[...]).astype(BF16)
    acc = h1
    yield
    for c in range(MLP_HIDDEN // D_MODEL):
        cs = slice(c * D_MODEL, (c + 1) * D_MODEL)
        up = jnp.maximum(jnp.dot(xn, wup_ref[:, cs], preferred_element_type=F32), 0.0)
        yield
        acc = acc + jnp.dot((up * up).astype(BF16), wdown_ref[cs, :], preferred_element_type=F32)
        yield
    if final_norm:
        acc = _rms(acc, nfin_ref[...])
    out_ref[...] = acc


GDN_PHASE_ORDER = "rmrmrmrmrmrmrmrmrmrmr"
HGRN_PHASE_ORDER = "rmmrmmrmmrmmrmrm"


def _rec_out_kernel(*refs, gdn, n_chunks, tiles_per_seq, n_tiles, final_norm):
    n_rec = 4 if gdn else 5
    rec_refs = refs[:n_rec]
    (sg_ref, h_ref, mw_ref, wout_ref, nmlp_ref, wup_ref, wdown_ref, nfin_ref,
     out_ref, s_ref, o_buf) = refs[n_rec:]
    step = pl.program_id(0)

    @pl.when(step == 0)
    def _():
        o_buf[...] = jnp.zeros_like(o_buf)

    slot = step % 2
    tile = jnp.minimum(step, n_tiles - 1)
    rec = _gdn_rec_tile if gdn else _hgrn_rec_tile
    stages = {
        "r": rec(*rec_refs, o_buf.at[slot], s_ref, n_chunks=n_chunks,
                 first_tile=tile % tiles_per_seq == 0),
        "m": _out_mlp_tile(o_buf.at[1 - slot], sg_ref, h_ref, mw_ref, wout_ref, nmlp_ref,
                           wup_ref, wdown_ref, nfin_ref, out_ref, per_head_norm=gdn,
                           final_norm=final_norm),
    }
    done = set()
    for tag in GDN_PHASE_ORDER if gdn else HGRN_PHASE_ORDER:
        if next(stages[tag], "end") == "end":
            done.add(tag)
    assert done == {"r", "m"}, "phase order does not exhaust both stages"


def _rec_out(rec_inputs, gdn, sg, h, mix_norm_w, w_out, norm_mlp, w_up, w_down, norm_final,
             seq_len, final_norm):
    n = h.shape[0]
    rows = ROWS_REC
    n_tiles = n // rows
    cur_spec = pl.BlockSpec((HEADS, rows, HEAD_DIM),
                            lambda s: (0, jnp.minimum(s, n_tiles - 1), 0))
    prev_spec = pl.BlockSpec((rows, D_MODEL), lambda s: (jnp.maximum(s - 1, 0), 0))
    mw = mix_norm_w.reshape(1, -1).astype(F32)
    scratch = [pltpu.VMEM((HEADS, HEAD_DIM, HEAD_DIM), F32),
               pltpu.VMEM((2, HEADS, rows, HEAD_DIM), F32)]
    if gdn:
        rec_specs = [cur_spec] * 3 + [
            pl.BlockSpec((rows, LANES), lambda s: (jnp.minimum(s, n_tiles - 1), 0))]
    else:
        rec_inputs = tuple(rec_inputs) + (_sub_block_column_selector(),)
        rec_specs = [cur_spec] * 4 + [_resident((SUB * HEAD_DIM, CHUNK))]
    return pl.pallas_call(
        functools.partial(_rec_out_kernel, gdn=gdn, n_chunks=rows // CHUNK,
                          tiles_per_seq=seq_len // rows, n_tiles=n_tiles, final_norm=final_norm),
        grid=(n_tiles + 1,),
        in_specs=rec_specs + [prev_spec, prev_spec,
                              _resident(mw.shape), _resident((D_MODEL, D_MODEL)),
                              _resident((1, D_MODEL)), _resident((D_MODEL, MLP_HIDDEN)),
                              _resident((MLP_HIDDEN, D_MODEL)), _resident((1, D_MODEL))],
        out_specs=prev_spec,
        out_shape=jax.ShapeDtypeStruct((n, D_MODEL), F32),
        scratch_shapes=scratch,
        compiler_params=_params(1),
        name="gdn_rec_out" if gdn else "hgrn_rec_out",
    )(*rec_inputs, sg, h, mw, w_out.astype(BF16), norm_mlp.reshape(1, D_MODEL),
      w_up.astype(BF16), w_down.astype(BF16), norm_final.reshape(1, D_MODEL))


def kernel(x, gdn_w_in, gdn_conv, gdn_a_log, gdn_dt_bias, gdn_onorm, gdn_w_out, hgrn_w_in, hgrn_lb_logits, hgrn_gnorm, hgrn_w_out, norm_mix, norm_mlp, mlp_w_up, mlp_w_down, norm_final):
    batch, seq_len, d_model = x.shape
    depth = norm_mix.shape[0]
    assert d_model == D_MODEL and seq_len % max(ROWS_IN, ROWS_REC) == 0
    h = x.reshape(batch * seq_len, d_model)
    for i in range(depth):
        j = i // 2
        gdn = i % 2 == 0
        if gdn:
            q, k, v, sg, gb = _gdn_in(h, norm_mix[i], gdn_w_in[j], gdn_conv[j],
                                      gdn_a_log[j], gdn_dt_bias[j], seq_len)
            rec_inputs = (q, k, v, gb)
            mix_norm_w, w_out = gdn_onorm[j], gdn_w_out[j]
        else:
            q, k, v, g, sg = _hgrn_in(h, norm_mix[i], hgrn_w_in[j], hgrn_lb_logits, i)
            rec_inputs = (q, k, v, g)
            mix_norm_w, w_out = hgrn_gnorm[j], hgrn_w_out[j]
        h = _rec_out(rec_inputs, gdn, sg, h, mix_norm_w, w_out, norm_mlp[i], mlp_w_up[i],
                     mlp_w_down[i], norm_final, seq_len, final_norm=(i == depth - 1))
    return h.reshape(batch, seq_len, d_model)
```
